```python
import jax
import jax.numpy as jnp
from jax import lax
import numpy as np

D_MODEL = 4096
BATCH = 1
SEQ = 8192
DEPTH = 2
DEC_BATCH = 8
DEC_SEQ = 16
PAST_LEN = 2048

CHUNK = 64
NH_A = 8
DK_A = 128
DV_A = 256
CONV_W = 4
NH_B = 4
DK_B = 256
DV_B = 512
GATE_RANK = 16
GLA_NORM = 16.0
NH_C = 16
DK_C = 128
DV_C = 128
N_BRANCH = 3
N_GROUPS = 4
EXPERTS_PER_GROUP = 8
N_EXPERTS = N_GROUPS * EXPERTS_PER_GROUP
TOP_K = 2
D_EXPERT = 1024
MOE_BLOCK = 128
NORM_EPS = 1e-6
NEG_BIG = -1e30
F_TINY = 1e-30

QK_A = NH_A * DK_A
W_A = NH_A * DV_A
QK_B = NH_B * DK_B
W_B = NH_B * DV_B
QK_C = NH_C * DK_C
W_C = NH_C * DV_C
IN_SIZES = (2 * QK_A, W_A, 2 * NH_A, W_A,
            QK_B, QK_B, W_B, GATE_RANK, W_B,
            QK_C, QK_C, W_C, W_C,
            N_BRANCH * D_MODEL)
N_IN_COLS = sum(IN_SIZES)

kernel_name = 'hybrid_streaming_encoder_step'


def rmsnorm(x, g):
    xf = x.astype(jnp.float32)
    y = xf * lax.rsqrt(jnp.mean(jnp.square(xf), axis=-1, keepdims=True) + NORM_EPS)
    return (y * g.astype(jnp.float32)).astype(x.dtype)


def head_rmsnorm(h, g):
    H, d = h.shape[-2:]
    return rmsnorm(h, g.reshape(H, d))


def split_cols(p):
    outs, o = [], 0
    for s in IN_SIZES:
        outs.append(p[..., o:o + s])
        o += s
    return outs


def chunk_len(T):
    return CHUNK if T % CHUNK == 0 else T


def to_chunks(x, L):
    B, T, H = x.shape[:3]
    x = x.reshape((B, T // L, L, H) + x.shape[3:])
    return jnp.moveaxis(x, (1, 3), (0, 2))


def from_chunks(x):
    x = jnp.moveaxis(x, (0, 2), (1, 3))
    B, nC, L, H = x.shape[:4]
    return x.reshape((B, nC * L, H) + x.shape[4:])


def causal_conv(u, buf, w, b):
    T = u.shape[1]
    up = jnp.concatenate([buf.astype(u.dtype), u], axis=1)
    y = b + sum(up[:, j:j + T] * w[j] for j in range(CONV_W))
    return y, up[:, T:]


def mlstm_chunked(q, k, v, ig, lf, C0, n0, m0):
    T = q.shape[1]
    L = chunk_len(T)
    causal = jnp.tril(jnp.ones((L, L), dtype=bool))

    def step(carry, inp):
        C, n, m = carry
        qc, kc, vc, igc, lfc = inp
        b = jnp.cumsum(lfc, axis=-1)
        logD = jnp.where(causal, b[..., :, None] - b[..., None, :] + igc[..., None, :], NEG_BIG)
        inter = b + m[..., None]
        mt = jnp.maximum(jnp.max(logD, axis=-1), inter)
        Dm = jnp.exp(logD - mt[..., None])
        sc = jnp.exp(inter - mt)
        S = jnp.einsum('bhtd,bhsd->bhts', qc, kc) * Dm
        num = sc[..., None] * jnp.einsum('bhvd,bhtd->bhtv', C, qc) + jnp.einsum('bhts,bhsv->bhtv', S, vc)
        den = sc * jnp.einsum('bhd,bhtd->bht', n, qc) + jnp.sum(S, axis=-1)
        hc = num / jnp.maximum(jnp.abs(den), jnp.exp(-mt))[..., None]
        wL, sL = Dm[..., -1, :], sc[..., -1]
        C = sL[..., None, None] * C + jnp.einsum('bhs,bhsv,bhsd->bhvd', wL, vc, kc)
        n = sL[..., None] * n + jnp.einsum('bhs,bhsd->bhd', wL, kc)
        return (C, n, mt[..., -1]), hc

    xs = (to_chunks(q, L), to_chunks(k, L), to_chunks(v, L),
          to_chunks(ig[..., None], L)[..., 0], to_chunks(lf[..., None], L)[..., 0])
    carry0 = (C0.astype(jnp.float32), n0.astype(jnp.float32), m0.astype(jnp.float32))
    (C, n, m), h = lax.scan(step, carry0, xs)
    return from_chunks(h), C, n, m


def gla_chunked(q, k, v, lg, S0):
    T = q.shape[1]
    L = chunk_len(T)
    causal = jnp.tril(jnp.ones((L, L), dtype=bool))[..., None]

    def step(S, inp):
        qc, kc, vc, gc = inp
        b = jnp.cumsum(gc, axis=2)
        dec = jnp.exp(jnp.where(causal, b[:, :, :, None, :] - b[:, :, None, :, :], NEG_BIG))
        A = jnp.einsum('bhtd,bhsd,bhtsd->bhts', qc, kc, dec)
        o = jnp.einsum('bhts,bhsv->bhtv', A, vc) + jnp.einsum('bhtd,bhdv->bhtv', qc * jnp.exp(b), S)
        bL = b[:, :, -1]
        S = jnp.exp(bL)[..., None] * S + jnp.einsum('bhsd,bhsv->bhdv', kc * jnp.exp(bL[:, :, None] - b), vc)
        return S, o

    xs = (to_chunks(q, L), to_chunks(k, L), to_chunks(v, L), to_chunks(lg, L))
    S, o = lax.scan(step, S0.astype(jnp.float32), xs)
    return from_chunks(o), S


def moe_dispatch(xf, eid, wts, w_exp1, w_exp3, w_exp2):
    N, D = xf.shape
    A = N * TOP_K
    nb = -(-A // MOE_BLOCK) + N_EXPERTS
    flat_e = eid.reshape(A)
    order = jnp.argsort(flat_e)
    se = flat_e[order]
    counts = jnp.bincount(flat_e, length=N_EXPERTS)
    padded = (counts + MOE_BLOCK - 1) // MOE_BLOCK * MOE_BLOCK
    pend = jnp.cumsum(padded)
    pstart = pend - padded
    start = jnp.cumsum(counts) - counts
    dest = pstart[se] + jnp.arange(A, dtype=jnp.int32) - start[se]
    rows = jnp.full((nb * MOE_BLOCK,), N, dtype=jnp.int32).at[dest].set((order // TOP_K).astype(jnp.int32))
    roww = jnp.zeros((nb * MOE_BLOCK,), jnp.float32).at[dest].set(wts.reshape(A)[order])
    blk_e = jnp.minimum(jnp.searchsorted(pend, jnp.arange(nb, dtype=jnp.int32) * MOE_BLOCK, side='right'),
                        N_EXPERTS - 1)
    xpad = jnp.concatenate([xf, jnp.zeros((1, D), xf.dtype)], axis=0)

    def run_block(args):
        r, e = args
        xb = xpad[r]
        hb = jax.nn.silu(xb @ w_exp1[e]) * (xb @ w_exp3[e])
        return hb @ w_exp2[e]

    yb = lax.map(run_block, (rows.reshape(nb, MOE_BLOCK), blk_e))
    y = jax.ops.segment_sum(yb.reshape(-1, D) * roww[:, None].astype(yb.dtype), rows, num_segments=N + 1)
    return y[:N]


def hier_moe(h, w_rg, b_rg, w_re, b_re, w_exp1, w_exp3, w_exp2):
    B, T, D = h.shape
    N = B * T
    xf = h.reshape(N, D)
    pg = jax.nn.softmax((xf @ w_rg + b_rg).astype(jnp.float32), axis=-1)
    grp = jnp.argmax(pg, axis=-1)
    p_grp = jnp.max(pg, axis=-1)
    le = (xf @ w_re + b_re).astype(jnp.float32).reshape(N, N_GROUPS, EXPERTS_PER_GROUP)
    le_g = le[jnp.arange(N), grp]
    top_logit, top_j = lax.top_k(le_g, TOP_K)
    wts = jax.nn.softmax(top_logit, axis=-1) * p_grp[:, None]
    eid = (grp[:, None] * EXPERTS_PER_GROUP + top_j).astype(jnp.int32)
    return moe_dispatch(xf, eid, wts, w_exp1, w_exp3, w_exp2).reshape(B, T, D)


def layer(x, c, conv_buf, a_C, a_n, a_m, b_S, c_S, lb,
          w_ada, b_ada, norm_mix, w_in, conv_w, conv_b, b_gate_a, hn_a,
          w_gk2, b_gk, hn_b, hn_c, w_br_a, w_br_b, w_br_c, w_out,
          norm_moe, w_rg, b_rg, w_re, b_re, w_exp1, w_exp3, w_exp2):
    B, T, _ = x.shape
    f32 = jnp.float32
    mod = jax.nn.silu(c) @ w_ada + b_ada
    sh1, sc1, g1, sh2, sc2, g2 = [m[:, None, :] for m in jnp.split(mod, 6, axis=-1)]
    h = rmsnorm(x, norm_mix) * (1 + sc1) + sh1
    (a_qk, a_v, a_if, a_o, b_q, b_k, b_v, b_lr, b_g,
     c_q, c_f, c_i, c_g, g_logit) = split_cols(h @ w_in)

    qk, conv_new = causal_conv(a_qk, conv_buf, conv_w, conv_b)
    qk = jax.nn.silu(qk.astype(f32)).reshape(B, T, 2 * NH_A, DK_A)
    ig = a_if[..., :NH_A].astype(f32) + b_gate_a[0].astype(f32)
    lf = jax.nn.log_sigmoid(a_if[..., NH_A:].astype(f32) + b_gate_a[1].astype(f32))
    ha, a_C, a_n, a_m = mlstm_chunked(qk[:, :, :NH_A] * DK_A ** -0.5, qk[:, :, NH_A:],
                                      a_v.astype(f32).reshape(B, T, NH_A, DV_A), ig, lf, a_C, a_n, a_m)
    o_a = (head_rmsnorm(ha, hn_a).reshape(B, T, W_A) * jax.nn.sigmoid(a_o.astype(f32))).astype(x.dtype)

    lg = jax.nn.log_sigmoid((b_lr @ w_gk2 + b_gk).astype(f32)) / GLA_NORM
    hb, b_S = gla_chunked(b_q.astype(f32).reshape(B, T, NH_B, DK_B) * DK_B ** -0.5,
                          b_k.astype(f32).reshape(B, T, NH_B, DK_B),
                          b_v.astype(f32).reshape(B, T, NH_B, DV_B),
                          lg.reshape(B, T, NH_B, DK_B), b_S)
    o_b = (head_rmsnorm(hb, hn_b).reshape(B, T, W_B) * jax.nn.silu(b_g.astype(f32))).astype(x.dtype)

    lbf = lb.astype(f32)
    f_c = lbf + (1.0 - lbf) * jax.nn.sigmoid(c_f.astype(f32))
    lf_c = jnp.log(jnp.maximum(f_c, F_TINY))
    hc, c_S = gla_chunked(jax.nn.silu(c_q.astype(f32)).reshape(B, T, NH_C, DK_C),
                          (1.0 - f_c).reshape(B, T, NH_C, DK_C),
                          c_i.astype(f32).reshape(B, T, NH_C, DV_C),
                          lf_c.reshape(B, T, NH_C, DK_C), c_S)
    o_c = (head_rmsnorm(hc, hn_c).reshape(B, T, W_C) * jax.nn.silu(c_g.astype(f32))).astype(x.dtype)

    gates = jax.nn.sigmoid(g_logit.astype(f32)).astype(x.dtype).reshape(B, T, N_BRANCH, D_MODEL)
    merged = (gates[:, :, 0] * (o_a @ w_br_a) + gates[:, :, 1] * (o_b @ w_br_b)
              + gates[:, :, 2] * (o_c @ w_br_c))
    x = x + g1 * (merged @ w_out)

    h2 = rmsnorm(x, norm_moe) * (1 + sc2) + sh2
    x = x + g2 * hier_moe(h2, w_rg, b_rg, w_re, b_re, w_exp1, w_exp3, w_exp2)
    return x, a_C, a_n, a_m, conv_new, b_S, c_S


def setup_inputs(seed: int = 0) -> dict:
    key = jax.random.key(seed)
    ks = iter(jax.random.split(key, 48))

    def nrm(shape, scale):
        return jax.random.normal(next(ks), shape, jnp.float32) * scale

    D = D_MODEL
    inp = {}
    inp['x_prompt'] = nrm((BATCH, SEQ, D), 1.0)
    inp['x_sample'] = nrm((DEC_BATCH, DEC_SEQ, D), 1.0)
    inp['c_prompt'] = nrm((BATCH, D), 1.0)
    inp['c_sample'] = nrm((DEC_BATCH, D), 1.0)
    inp['state_a_C'] = nrm((DEPTH, DEC_BATCH, NH_A, DV_A, DK_A), 0.1)
    inp['state_a_n'] = nrm((DEPTH, DEC_BATCH, NH_A, DK_A), 0.1)
    inp['state_a_m'] = nrm((DEPTH, DEC_BATCH, NH_A), 1.0)
    inp['state_a_conv'] = nrm((DEPTH, DEC_BATCH, CONV_W - 1, 2 * QK_A), 1.0)
    inp['state_b_S'] = nrm((DEPTH, DEC_BATCH, NH_B, DK_B, DV_B), 0.1)
    inp['state_c_S'] = nrm((DEPTH, DEC_BATCH, NH_C, DK_C, DV_C), 0.1)
    inp['w_ada'] = nrm((DEPTH, D, 6 * D), 0.5 * D ** -0.5)
    inp['b_ada'] = nrm((DEPTH, 6 * D), 0.02)
    inp['norm_mix'] = 1.0 + nrm((DEPTH, D), 0.02)
    inp['w_in'] = nrm((DEPTH, D, N_IN_COLS), D ** -0.5)
    inp['conv_w'] = nrm((DEPTH, CONV_W, 2 * QK_A), CONV_W ** -0.5)
    inp['conv_b'] = nrm((DEPTH, 2 * QK_A), 0.02)
    inp['b_gate_a'] = jnp.stack([-1.0 + nrm((DEPTH, NH_A), 0.1),
                                 jnp.linspace(3.0, 6.0, NH_A) + nrm((DEPTH, NH_A), 0.1)], axis=1)
    inp['hn_a'] = 1.0 + nrm((DEPTH, W_A), 0.02)
    inp['w_gk2'] = nrm((DEPTH, GATE_RANK, QK_B), GATE_RANK ** -0.5)
    inp['b_gk'] = nrm((DEPTH, QK_B), 0.1)
    inp['hn_b'] = 1.0 + nrm((DEPTH, W_B), 0.02)
    inp['lb_logits'] = nrm((DEPTH, QK_C), 0.5)
    inp['hn_c'] = 1.0 + nrm((DEPTH, W_C), 0.02)
    inp['w_br_a'] = nrm((DEPTH, W_A, D), W_A ** -0.5)
    inp['w_br_b'] = nrm((DEPTH, W_B, D), W_B ** -0.5)
    inp['w_br_c'] = nrm((DEPTH, W_C, D), W_C ** -0.5)
    inp['w_out'] = nrm((DEPTH, D, D), D ** -0.5)
    inp['norm_moe'] = 1.0 + nrm((DEPTH, D), 0.02)
    inp['w_rg'] = nrm((DEPTH, D, N_GROUPS), D ** -0.5)
    inp['b_rg'] = nrm((DEPTH, N_GROUPS), 0.01)
    inp['w_re'] = nrm((DEPTH, D, N_EXPERTS), D ** -0.5)
    inp['b_re'] = nrm((DEPTH, N_EXPERTS), 0.01)
    inp['w_exp1'] = nrm((DEPTH, N_EXPERTS, D, D_EXPERT), D ** -0.5)
    inp['w_exp3'] = nrm((DEPTH, N_EXPERTS, D, D_EXPERT), D ** -0.5)
    inp['w_exp2'] = nrm((DEPTH, N_EXPERTS, D_EXPERT, D), D_EXPERT ** -0.5)
    inp['norm_final'] = 1.0 + nrm((D,), 0.02)
    return inp


def reference(x_prompt, x_sample, c_prompt, c_sample,
              state_a_C, state_a_n, state_a_m, state_a_conv, state_b_S, state_c_S,
              w_ada, b_ada, norm_mix, w_in, conv_w, conv_b, b_gate_a, hn_a,
              w_gk2, b_gk, hn_b, lb_logits, hn_c, w_br_a, w_br_b, w_br_c, w_out,
              norm_moe, w_rg, b_rg, w_re, b_re, w_exp1, w_exp3, w_exp2, norm_final):
    f32 = jnp.float32
    sm = jax.nn.softmax(lb_logits.astype(f32), axis=0)
    lb_all = jnp.cumsum(sm, axis=0) - sm[0]
    Bp = x_prompt.shape[0]
    xp, xs = x_prompt, x_sample
    P = [[] for _ in range(6)]
    S = [[] for _ in range(6)]
    for l in range(DEPTH):
        wl = (w_ada[l], b_ada[l], norm_mix[l], w_in[l], conv_w[l], conv_b[l], b_gate_a[l], hn_a[l],
              w_gk2[l], b_gk[l], hn_b[l], hn_c[l], w_br_a[l], w_br_b[l], w_br_c[l], w_out[l],
              norm_moe[l], w_rg[l], b_rg[l], w_re[l], b_re[l], w_exp1[l], w_exp3[l], w_exp2[l])
        xp, *new_p = layer(xp, c_prompt,
                           jnp.zeros((Bp, CONV_W - 1, 2 * QK_A), x_prompt.dtype),
                           jnp.zeros((Bp, NH_A, DV_A, DK_A), f32),
                           jnp.zeros((Bp, NH_A, DK_A), f32),
                           jnp.zeros((Bp, NH_A), f32),
                           jnp.zeros((Bp, NH_B, DK_B, DV_B), f32),
                           jnp.zeros((Bp, NH_C, DK_C, DV_C), f32),
                           lb_all[l], *wl)
        xs, *new_s = layer(xs, c_sample, state_a_conv[l], state_a_C[l], state_a_n[l], state_a_m[l],
                           state_b_S[l], state_c_S[l], lb_all[l], *wl)
        for lst, v in zip(P, new_p):
            lst.append(v)
        for lst, v in zip(S, new_s):
            lst.append(v)
    y_prompt = rmsnorm(xp, norm_final)
    y_sample = rmsnorm(xs, norm_final)
    return (y_prompt, y_sample,
            jnp.stack(P[0]), jnp.stack(P[1]), jnp.stack(P[2]), jnp.stack(P[3]), jnp.stack(P[4]), jnp.stack(P[5]),
            jnp.stack(S[0]), jnp.stack(S[1]), jnp.stack(S[2]), jnp.stack(S[3]), jnp.stack(S[4]), jnp.stack(S[5]))
```

```python
import functools

import jax
import jax.numpy as jnp
import numpy as np
from jax import lax
from jax.experimental import pallas as pl
from jax.experimental.pallas import tpu as pltpu

F32 = jnp.float32
BF16 = jnp.bfloat16

NORM_EPS = 1e-6
NEG_BIG = -1e30
F_TINY = 1e-30
GLA_NORM = 16.0
CONV_W = 4
TOP_K = 2
N_GROUPS = 4
EXPERTS_PER_GROUP = 8
N_EXPERTS = N_GROUPS * EXPERTS_PER_GROUP
N_BRANCH = 3

LANES = 128
MOD_ROWS = 16
VMEM_LIMIT = 56 * 1024 * 1024

TM_BIG = 1040
TM_SMALL = 520
TN_IN = 768
MOE_ROWS = 256
SUB = 16


def _cparams(sem):
    return pltpu.CompilerParams(dimension_semantics=sem, vmem_limit_bytes=VMEM_LIMIT)


def _dot(a, b):
    return jnp.dot(a, b, preferred_element_type=F32)


def _dot_nt(a, b):
    return lax.dot_general(a, b, (((1,), (1,)), ((), ())), preferred_element_type=F32)


def _dot_tn(a, b):
    return lax.dot_general(a, b, (((0,), (0,)), ((), ())), preferred_element_type=F32)


def _split3(x):
    hi = x.astype(BF16)
    r = x - hi.astype(F32)
    mid = r.astype(BF16)
    lo = (r - mid.astype(F32)).astype(BF16)
    return hi, mid, lo


def _dot_sel(sel, x):
    hi, mid, lo = _split3(x)
    return _dot(sel, hi) + _dot(sel, mid) + _dot(sel, lo)


def _dot_nt_sel(sel, x):
    hi, mid, lo = _split3(x)
    return _dot_nt(sel, hi) + _dot_nt(sel, mid) + _dot_nt(sel, lo)


def _dot_f32(a, b):
    a_hi = a.astype(BF16)
    a_lo = (a - a_hi.astype(F32)).astype(BF16)
    b_hi = b.astype(BF16)
    b_lo = (b - b_hi.astype(F32)).astype(BF16)
    return _dot(a_hi, b_hi) + _dot(a_hi, b_lo) + _dot(a_lo, b_hi)


def _sigmoid(x):
    return jax.nn.sigmoid(x)


def _silu(x):
    return x * jax.nn.sigmoid(x)


def _log_sigmoid(x):
    return jnp.minimum(x, 0.0) - jnp.log1p(jnp.exp(-jnp.abs(x)))


def _rms(x):
    return x * lax.rsqrt(jnp.mean(x * x, axis=-1, keepdims=True) + NORM_EPS)


def _tril(n):
    r = lax.broadcasted_iota(jnp.int32, (n, n), 0)
    c = lax.broadcasted_iota(jnp.int32, (n, n), 1)
    return jnp.where(r >= c, 1.0, 0.0).astype(BF16)


def _eye(n):
    r = lax.broadcasted_iota(jnp.int32, (n, n), 0)
    c = lax.broadcasted_iota(jnp.int32, (n, n), 1)
    return jnp.where(r == c, 1.0, 0.0).astype(BF16)


def _mod_kernel(c_ref, w_ref, b_ref, o_ref):
    c = c_ref[...]
    o_ref[0] = _dot(_silu(c).astype(BF16), w_ref[0].astype(BF16)) + b_ref[0]


def _modulation(c_all, w_ada, b_ada):
    depth, d, n = w_ada.shape
    tn = 1024
    return pl.pallas_call(
        _mod_kernel,
        grid=(depth, n // tn),
        in_specs=[pl.BlockSpec((MOD_ROWS, d), lambda l, j: (0, 0)),
                  pl.BlockSpec((1, d, tn), lambda l, j: (l, 0, j)),
                  pl.BlockSpec((1, 1, tn), lambda l, j: (l, 0, j))],
        out_specs=pl.BlockSpec((1, MOD_ROWS, tn), lambda l, j: (l, 0, j)),
        out_shape=jax.ShapeDtypeStruct((depth, MOD_ROWS, n), F32),
        compiler_params=_cparams(("arbitrary", "arbitrary")),
        name="modulation",
    )(c_all, w_ada, b_ada.reshape(depth, 1, n))


def _row_mod(mixed, r_ref, m_ref):
    if mixed:
        return _dot_sel(r_ref[...], m_ref[0])
    return m_ref[0, 0:1, :]


def _norm_mod_kernel(x_ref, r_ref, g_ref, sh_ref, sc_ref, o_ref):
    last = pl.num_programs(0) - 1

    def body(mixed):
        y = _rms(x_ref[...]) * g_ref[...]
        o_ref[...] = (y * (1.0 + _row_mod(mixed, r_ref, sc_ref)) + _row_mod(mixed, r_ref, sh_ref)
                      ).astype(o_ref.dtype)

    pl.when(pl.program_id(0) != last)(lambda: body(False))
    pl.when(pl.program_id(0) == last)(lambda: body(True))


def _norm_mod(x, rsel, gain, mod_l, sh_idx, sc_idx):
    m, d = x.shape
    tm = TM_SMALL
    return pl.pallas_call(
        _norm_mod_kernel,
        grid=(m // tm,),
        in_specs=[pl.BlockSpec((tm, d), lambda i: (i, 0)),
                  pl.BlockSpec((tm, MOD_ROWS), lambda i: (i, 0)),
                  pl.BlockSpec((1, d), lambda i: (0, 0)),
                  pl.BlockSpec((1, MOD_ROWS, d), lambda i: (0, 0, sh_idx)),
                  pl.BlockSpec((1, MOD_ROWS, d), lambda i: (0, 0, sc_idx))],
        out_specs=pl.BlockSpec((tm, d), lambda i: (i, 0)),
        out_shape=jax.ShapeDtypeStruct((m, d), BF16),
        compiler_params=_cparams(("arbitrary",)),
        name="norm_mod",
    )(x, rsel, gain.reshape(1, d), mod_l, mod_l)


def _norm_mod_router_kernel(x_ref, r_ref, g_ref, sh_ref, sc_ref, wr_ref, br_ref, o_ref, lg_ref):
    last = pl.num_programs(0) - 1

    def body(mixed):
        y = _rms(x_ref[...]) * g_ref[...]
        h = y * (1.0 + _row_mod(mixed, r_ref, sc_ref)) + _row_mod(mixed, r_ref, sh_ref)
        hb = h.astype(o_ref.dtype)
        o_ref[...] = hb
        lg_ref[...] = _dot(hb, wr_ref[...].astype(BF16)) + br_ref[...]

    pl.when(pl.program_id(0) != last)(lambda: body(False))
    pl.when(pl.program_id(0) == last)(lambda: body(True))


def _norm_mod_router(x, rsel, gain, mod_l, sh_idx, sc_idx, w_r, b_r):
    m, d = x.shape
    tm = TM_SMALL
    return pl.pallas_call(
        _norm_mod_router_kernel,
        grid=(m // tm,),
        in_specs=[pl.BlockSpec((tm, d), lambda i: (i, 0)),
                  pl.BlockSpec((tm, MOD_ROWS), lambda i: (i, 0)),
                  pl.BlockSpec((1, d), lambda i: (0, 0)),
                  pl.BlockSpec((1, MOD_ROWS, d), lambda i: (0, 0, sh_idx)),
                  pl.BlockSpec((1, MOD_ROWS, d), lambda i: (0, 0, sc_idx)),
                  pl.BlockSpec((d, LANES), lambda i: (0, 0)),
                  pl.BlockSpec((1, LANES), lambda i: (0, 0))],
        out_specs=[pl.BlockSpec((tm, d), lambda i: (i, 0)),
                   pl.BlockSpec((tm, LANES), lambda i: (i, 0))],
        out_shape=[jax.ShapeDtypeStruct((m, d), BF16),
                   jax.ShapeDtypeStruct((m, LANES), F32)],
        compiler_params=_cparams(("arbitrary",)),
        name="norm_mod_router",
    )(x, rsel, gain.reshape(1, d), mod_l, mod_l, w_r, b_r)


def _final_norm_kernel(x_ref, g_ref, o_ref):
    o_ref[...] = _rms(x_ref[...]) * g_ref[...]


def _final_norm(x, gain, row0, rows, tm):
    d = x.shape[1]
    blk0 = row0 // tm
    return pl.pallas_call(
        _final_norm_kernel,
        grid=(rows // tm,),
        in_specs=[pl.BlockSpec((tm, d), lambda i: (blk0 + i, 0)),
                  pl.BlockSpec((1, d), lambda i: (0, 0))],
        out_specs=pl.BlockSpec((tm, d), lambda i: (i, 0)),
        out_shape=jax.ShapeDtypeStruct((rows, d), F32),
        compiler_params=_cparams(("arbitrary",)),
        name="final_norm",
    )(x, gain.reshape(1, d))


def _matmul_kernel(a_ref, b_ref, o_ref):
    o_ref[...] = _dot(a_ref[...], b_ref[...]).astype(o_ref.dtype)


def _matmul(a, b, out_dtype, tm, tn):
    m, k = a.shape
    n = b.shape[1]
    return pl.pallas_call(
        _matmul_kernel,
        grid=(m // tm, n // tn),
        in_specs=[pl.BlockSpec((tm, k), lambda i, j: (i, 0)),
                  pl.BlockSpec((k, tn), lambda i, j: (0, j))],
        out_specs=pl.BlockSpec((tm, tn), lambda i, j: (i, j)),
        out_shape=jax.ShapeDtypeStruct((m, n), out_dtype),
        compiler_params=_cparams(("arbitrary", "arbitrary")),
        name="in_proj",
    )(a, b)


def _merge_kernel(oa_ref, ob_ref, oc_ref, wa_ref, wb_ref, wc_ref, ga_ref, gb_ref, gc_ref, o_ref):
    acc = _sigmoid(ga_ref[...]) * _dot(oa_ref[...], wa_ref[...])
    acc = acc + _sigmoid(gb_ref[...]) * _dot(ob_ref[...], wb_ref[...])
    acc = acc + _sigmoid(gc_ref[...]) * _dot(oc_ref[...], wc_ref[...])
    o_ref[...] = acc.astype(o_ref.dtype)


def _merge(o_a, o_b, o_c, w_a, w_b, w_c, p, gate_col0):
    m, kw = o_a.shape
    d = w_a.shape[1]
    tm, tn = TM_SMALL, 512
    g0 = gate_col0 // tn
    gstride = d // tn
    o_spec = pl.BlockSpec((tm, kw), lambda j, i: (i, 0))
    w_spec = pl.BlockSpec((kw, tn), lambda j, i: (0, j))

    def g_spec(br):
        return pl.BlockSpec((tm, tn), lambda j, i: (i, g0 + br * gstride + j))

    return pl.pallas_call(
        _merge_kernel,
        grid=(d // tn, m // tm),
        in_specs=[o_spec, o_spec, o_spec, w_spec, w_spec, w_spec, g_spec(0), g_spec(1), g_spec(2)],
        out_specs=pl.BlockSpec((tm, tn), lambda j, i: (i, j)),
        out_shape=jax.ShapeDtypeStruct((m, d), BF16),
        compiler_params=_cparams(("arbitrary", "arbitrary")),
        name="merge",
    )(o_a, o_b, o_c, w_a, w_b, w_c, p, p, p)


def _out_proj_kernel(a_ref, w_ref, x_ref, r_ref, g_ref, o_ref):
    last = pl.num_programs(0) - 1
    acc = _dot(a_ref[...], w_ref[...])

    def body(mixed):
        o_ref[...] = x_ref[...] + _row_mod(mixed, r_ref, g_ref) * acc

    pl.when(pl.program_id(0) != last)(lambda: body(False))
    pl.when(pl.program_id(0) == last)(lambda: body(True))


def _out_proj(a, w, x, rsel, mod_l, g_idx):
    m, k = a.shape
    d = w.shape[1]
    tm, tn = TM_BIG, 512
    nj = d // tn
    return pl.pallas_call(
        _out_proj_kernel,
        grid=(m // tm, nj),
        in_specs=[pl.BlockSpec((tm, k), lambda i, j: (i, 0)),
                  pl.BlockSpec((k, tn), lambda i, j: (0, j)),
                  pl.BlockSpec((tm, tn), lambda i, j: (i, j)),
                  pl.BlockSpec((tm, MOD_ROWS), lambda i, j: (i, 0)),
                  pl.BlockSpec((1, MOD_ROWS, tn), lambda i, j: (0, 0, g_idx * nj + j))],
        out_specs=pl.BlockSpec((tm, tn), lambda i, j: (i, j)),
        out_shape=jax.ShapeDtypeStruct((m, d), F32),
        compiler_params=_cparams(("arbitrary", "arbitrary")),
        name="out_proj",
    )(a, w, x, rsel, mod_l)


def _gated_add_kernel(x_ref, y0_ref, y1_ref, r_ref, g_ref, o_ref):
    last = pl.num_programs(0) - 1

    def body(mixed):
        o_ref[...] = x_ref[...] + _row_mod(mixed, r_ref, g_ref) * (y0_ref[...] + y1_ref[...])

    pl.when(pl.program_id(0) != last)(lambda: body(False))
    pl.when(pl.program_id(0) == last)(lambda: body(True))


def _gated_add(x, y0, y1, rsel, mod_l, g_idx):
    m, d = x.shape
    tm, tn = TM_SMALL, 1024
    nj = d // tn
    spec = pl.BlockSpec((tm, tn), lambda i, j: (i, j))
    return pl.pallas_call(
        _gated_add_kernel,
        grid=(m // tm, nj),
        in_specs=[spec, spec, spec,
                  pl.BlockSpec((tm, MOD_ROWS), lambda i, j: (i, 0)),
                  pl.BlockSpec((1, MOD_ROWS, tn), lambda i, j: (0, 0, g_idx * nj + j))],
        out_specs=spec,
        out_shape=jax.ShapeDtypeStruct((m, d), F32),
        compiler_params=_cparams(("arbitrary", "arbitrary")),
        name="moe_combine",
    )(x, y0, y1, rsel, mod_l)


def _mlstm_kernel(qp_ref, kp_ref, v_ref, og_ref, if_ref, conv0_ref, c0_ref, n0_ref, m0_ref,
                  cw_ref, cb_ref, bif_ref, hn_ref,
                  o_ref, cout_ref, nout_ref, mout_ref, convout_ref,
                  c_s, n_s, m_s, ubuf, q_s, k_s, *, L, NH, DK, DV):
    ci = pl.program_id(1)
    QK = NH * DK

    @pl.when(ci == 0)
    def _():
        c_s[...] = c0_ref[0]
        n_s[...] = n0_ref[0]
        m_s[...] = m0_ref[0]
        ubuf[8 - (CONV_W - 1):8, :] = conv0_ref[0]

    ubuf[8:8 + L, 0:QK] = qp_ref[...]
    ubuf[8:8 + L, QK:2 * QK] = kp_ref[...]
    y = cb_ref[...]
    for j in range(CONV_W):
        y = y + ubuf[8 - (CONV_W - 1) + j:8 - (CONV_W - 1) + j + L, :] * cw_ref[j:j + 1, :]
    ubuf[0:8, :] = ubuf[L:L + 8, :]
    qk = _silu(y)
    q_s[...] = qk[:, 0:QK] * (DK ** -0.5)
    k_s[...] = qk[:, QK:2 * QK]

    ifv = if_ref[...] + bif_ref[...]
    b_all = _dot_sel(_tril(L), _log_sigmoid(ifv))
    eye = _eye(LANES)
    ig_t = _dot_nt_sel(eye, ifv)
    b_t = _dot_nt_sel(eye, b_all)
    row = lax.broadcasted_iota(jnp.int32, (L, L), 0)
    col = lax.broadcasted_iota(jnp.int32, (L, L), 1)
    causal = row >= col
    lane = lax.broadcasted_iota(jnp.int32, (1, LANES), 1)
    m_old = m_s[...]
    m_new = m_old

    for h in range(NH):
        b_c = b_all[:, NH + h:NH + h + 1]
        ig_c = ifv[:, h:h + 1]
        b_r = b_t[NH + h:NH + h + 1, :]
        ig_r = ig_t[h:h + 1, :]
        m_prev = m_old[:, h:h + 1]
        log_d = jnp.where(causal, b_c - b_r + ig_r, NEG_BIG)
        inter = b_c + m_prev
        mt = jnp.maximum(jnp.max(log_d, axis=-1, keepdims=True), inter)
        dm = jnp.exp(log_d - mt)
        sc = jnp.exp(inter - mt)
        qh = q_s[:, h * DK:(h + 1) * DK]
        kh = k_s[:, h * DK:(h + 1) * DK]
        vh = v_ref[:, h * DV:(h + 1) * DV]
        qb, kb, vb = qh.astype(BF16), kh.astype(BF16), vh.astype(BF16)
        c_h = c_s[h]
        n_h = n_s[h:h + 1, :]
        s = _dot_nt(qb, kb) * dm
        num = sc * _dot_nt(qb, c_h.astype(BF16)) + _dot(s.astype(BF16), vb)
        den = sc * jnp.sum(qh * n_h, axis=-1, keepdims=True) + jnp.sum(s, axis=-1, keepdims=True)
        hc = num / jnp.maximum(jnp.abs(den), jnp.exp(-mt))
        mt_l = mt[L - 1:L, :]
        w_l = jnp.exp((b_c[L - 1:L, :] - b_c) + ig_c - mt_l)
        s_l = sc[L - 1:L, :]
        c_s[h] = s_l * c_h + _dot_tn((vh * w_l).astype(BF16), kb)
        n_s[h:h + 1, :] = s_l * n_h + jnp.sum(w_l * kh, axis=0, keepdims=True)
        m_new = jnp.where(lane == h, mt_l, m_new)
        out = _rms(hc) * hn_ref[:, h * DV:(h + 1) * DV] * _sigmoid(og_ref[:, h * DV:(h + 1) * DV])
        o_ref[:, h * DV:(h + 1) * DV] = out.astype(o_ref.dtype)

    m_s[...] = m_new

    @pl.when(ci == pl.num_programs(1) - 1)
    def _():
        cout_ref[0] = c_s[...]
        nout_ref[0] = n_s[...]
        mout_ref[0] = m_s[...]
        convout_ref[0] = ubuf[8 - (CONV_W - 1):8, :]


def _mlstm(p, cols, conv0, c0, n0, m0, conv_w, conv_b, bias_if, hn, *, row0, B, T, L, NH, DK, DV):
    QK, W = NH * DK, NH * DV
    nC = T // L
    rb0 = row0 // L
    c_qk, c_v, c_o, c_if = cols

    def rows(b, c):
        return rb0 + b * nC + c

    kern = functools.partial(_mlstm_kernel, L=L, NH=NH, DK=DK, DV=DV)
    return pl.pallas_call(
        kern,
        grid=(B, nC),
        in_specs=[pl.BlockSpec((L, QK), lambda b, c: (rows(b, c), c_qk // QK)),
                  pl.BlockSpec((L, QK), lambda b, c: (rows(b, c), c_qk // QK + 1)),
                  pl.BlockSpec((L, W), lambda b, c: (rows(b, c), c_v // W)),
                  pl.BlockSpec((L, W), lambda b, c: (rows(b, c), c_o // W)),
                  pl.BlockSpec((L, LANES), lambda b, c: (rows(b, c), c_if // LANES)),
                  pl.BlockSpec((1, CONV_W - 1, 2 * QK), lambda b, c: (b, 0, 0)),
                  pl.BlockSpec((1, NH, DV, DK), lambda b, c: (b, 0, 0, 0)),
                  pl.BlockSpec((1, NH, DK), lambda b, c: (b, 0, 0)),
                  pl.BlockSpec((1, 1, LANES), lambda b, c: (b, 0, 0)),
                  pl.BlockSpec((CONV_W, 2 * QK), lambda b, c: (0, 0)),
                  pl.BlockSpec((1, 2 * QK), lambda b, c: (0, 0)),
                  pl.BlockSpec((1, LANES), lambda b, c: (0, 0)),
                  pl.BlockSpec((1, W), lambda b, c: (0, 0))],
        out_specs=[pl.BlockSpec((L, W), lambda b, c: (b * nC + c, 0)),
                   pl.BlockSpec((1, NH, DV, DK), lambda b, c: (b, 0, 0, 0)),
                   pl.BlockSpec((1, NH, DK), lambda b, c: (b, 0, 0)),
                   pl.BlockSpec((1, 1, LANES), lambda b, c: (b, 0, 0)),
                   pl.BlockSpec((1, CONV_W - 1, 2 * QK), lambda b, c: (b, 0, 0))],
        out_shape=[jax.ShapeDtypeStruct((B * T, W), BF16),
                   jax.ShapeDtypeStruct((B, NH, DV, DK), F32),
                   jax.ShapeDtypeStruct((B, NH, DK), F32),
                   jax.ShapeDtypeStruct((B, 1, LANES), F32),
                   jax.ShapeDtypeStruct((B, CONV_W - 1, 2 * QK), F32)],
        scratch_shapes=[pltpu.VMEM((NH, DV, DK), F32),
                        pltpu.VMEM((NH, DK), F32),
                        pltpu.VMEM((1, LANES), F32),
                        pltpu.VMEM((L + 8, 2 * QK), F32),
                        pltpu.VMEM((L, QK), F32),
                        pltpu.VMEM((L, QK), F32)],
        compiler_params=_cparams(("arbitrary", "arbitrary")),
        name="mixer_mlstm",
    )(p, p, p, p, p, conv0, c0, n0, m0, conv_w, conv_b, bias_if, hn)


def _gla_chunk(q, k, v, g, st, L):
    dk = q.shape[1]
    c = min(SUB, L)
    nsub = L // c
    b = _dot_sel(_tril(L), g)
    b_l = b[L - 1:L, :]
    o = _dot_nt((q * jnp.exp(b)).astype(BF16), st.astype(BF16))

    q3 = q.reshape(nsub, c, dk)
    k3 = k.reshape(nsub, c, dk)
    b3 = b.reshape(nsub, c, dk)
    t_idx = lax.broadcasted_iota(jnp.int32, (1, c, 1), 1)
    s_idx = lax.broadcasted_iota(jnp.int32, (1, 1, c), 2)
    a_diag = jnp.zeros((nsub, c, c), F32)
    for s in range(c):
        arg = jnp.where(t_idx >= s, b3 - b3[:, s:s + 1, :], NEG_BIG)
        col_s = jnp.sum(q3 * k3[:, s:s + 1, :] * jnp.exp(arg), axis=-1, keepdims=True)
        a_diag = jnp.where(s_idx == s, col_s, a_diag)
    a_diag = a_diag.reshape(L, c)

    row = lax.broadcasted_iota(jnp.int32, (L, L), 0)
    col = lax.broadcasted_iota(jnp.int32, (L, L), 1)
    if nsub == 1:
        a = a_diag
    else:
        rep_r = lax.broadcasted_iota(jnp.int32, (c, L), 0)
        rep_c = lax.broadcasted_iota(jnp.int32, (c, L), 1)
        rep = jnp.where((rep_c & (c - 1)) == rep_r, 1.0, 0.0).astype(BF16)
        a = jnp.where((row & -c) == (col & -c), _dot(a_diag.astype(BF16), rep), 0.0)
        blocks = [jnp.zeros((c, L), F32)]
        for i in range(1, nsub):
            r_i = b[i * c - 1:i * c, :]
            q_i = q[i * c:(i + 1) * c, :] * jnp.exp(b[i * c:(i + 1) * c, :] - r_i)
            k_i = k * jnp.exp(jnp.minimum(r_i - b, 0.0))
            blocks.append(_dot_nt(q_i.astype(BF16), k_i.astype(BF16)))
        a = a + jnp.where((col & -c) < (row & -c), jnp.concatenate(blocks, axis=0), 0.0)
    vb = v.astype(BF16)
    o = o + _dot(a.astype(BF16), vb)
    st_new = st * jnp.exp(b_l) + _dot_tn(vb, (k * jnp.exp(b_l - b)).astype(BF16))
    return o, st_new


def _gla_b_kernel(q_ref, k_ref, v_ref, gt_ref, lr_ref, s0_ref, w2_ref, bgk_ref, hn_ref,
                  o_ref, sout_ref, st_s, *, L, DK):
    ci = pl.program_id(2)

    @pl.when(ci == 0)
    def _():
        st_s[...] = s0_ref[0, 0]

    lg = _log_sigmoid(_dot_f32(lr_ref[...], w2_ref[...]) + bgk_ref[...]) / GLA_NORM
    o, st_new = _gla_chunk(q_ref[...] * (DK ** -0.5), k_ref[...], v_ref[...], lg, st_s[...], L)
    st_s[...] = st_new
    o_ref[...] = (_rms(o) * hn_ref[...] * _silu(gt_ref[...])).astype(o_ref.dtype)

    @pl.when(ci == pl.num_programs(2) - 1)
    def _():
        sout_ref[0, 0] = st_s[...]


def _gla_b(p, cols, s0t, w_gk2p, b_gk, hn, *, row0, B, T, L, NH, DK, DV):
    nC = T // L
    rb0 = row0 // L
    c_q, c_k, c_v, c_g, c_lr = cols

    def rows(b, c):
        return rb0 + b * nC + c

    kern = functools.partial(_gla_b_kernel, L=L, DK=DK)
    return pl.pallas_call(
        kern,
        grid=(B, NH, nC),
        in_specs=[pl.BlockSpec((L, DK), lambda b, h, c: (rows(b, c), c_q // DK + h)),
                  pl.BlockSpec((L, DK), lambda b, h, c: (rows(b, c), c_k // DK + h)),
                  pl.BlockSpec((L, DV), lambda b, h, c: (rows(b, c), c_v // DV + h)),
                  pl.BlockSpec((L, DV), lambda b, h, c: (rows(b, c), c_g // DV + h)),
                  pl.BlockSpec((L, LANES), lambda b, h, c: (rows(b, c), c_lr // LANES)),
                  pl.BlockSpec((1, 1, DV, DK), lambda b, h, c: (b, h, 0, 0)),
                  pl.BlockSpec((LANES, DK), lambda b, h, c: (0, h)),
                  pl.BlockSpec((1, DK), lambda b, h, c: (0, h)),
                  pl.BlockSpec((1, DV), lambda b, h, c: (0, h))],
        out_specs=[pl.BlockSpec((L, DV), lambda b, h, c: (b * nC + c, h)),
                   pl.BlockSpec((1, 1, DV, DK), lambda b, h, c: (b, h, 0, 0))],
        out_shape=[jax.ShapeDtypeStruct((B * T, NH * DV), BF16),
                   jax.ShapeDtypeStruct((B, NH, DV, DK), F32)],
        scratch_shapes=[pltpu.VMEM((DV, DK), F32)],
        compiler_params=_cparams(("arbitrary", "arbitrary", "arbitrary")),
        name="mixer_gla",
    )(p, p, p, p, p, s0t, w_gk2p, b_gk, hn)


def _gla_c_kernel(q_ref, f_ref, i_ref, gt_ref, s0_ref, lbl_ref, hn_ref,
                  o_ref, sout_ref, st_s, *, L, HP, DK, DV, layer):
    ci = pl.program_id(2)

    @pl.when(ci == 0)
    def _():
        st_s[...] = s0_ref[0]

    lbl = lbl_ref[...]
    e = jnp.exp(lbl - jnp.max(lbl, axis=0, keepdims=True))
    sm = e / jnp.sum(e, axis=0, keepdims=True)
    lb = jnp.sum(sm[0:layer + 1, :], axis=0, keepdims=True) - sm[0:1, :]

    for h in range(HP):
        sl = slice(h * DK, (h + 1) * DK)
        sv = slice(h * DV, (h + 1) * DV)
        lbh = lb[:, sl]
        f = lbh + (1.0 - lbh) * _sigmoid(f_ref[:, sl])
        lg = jnp.log(jnp.maximum(f, F_TINY))
        o, st_new = _gla_chunk(_silu(q_ref[:, sl]), 1.0 - f, i_ref[:, sv], lg, st_s[h], L)
        st_s[h] = st_new
        o_ref[:, sv] = (_rms(o) * hn_ref[:, sv] * _silu(gt_ref[:, sv])).astype(o_ref.dtype)

    @pl.when(ci == pl.num_programs(2) - 1)
    def _():
        sout_ref[0] = st_s[...]


def _gla_c(p, cols, s0t, lb_logits, hn, *, layer, row0, B, T, L, NH, DK, DV, HP):
    nC = T // L
    rb0 = row0 // L
    c_q, c_f, c_i, c_g = cols
    wq, wv = HP * DK, HP * DV

    def rows(b, c):
        return rb0 + b * nC + c

    kern = functools.partial(_gla_c_kernel, L=L, HP=HP, DK=DK, DV=DV, layer=layer)
    return pl.pallas_call(
        kern,
        grid=(B, NH // HP, nC),
        in_specs=[pl.BlockSpec((L, wq), lambda b, g, c: (rows(b, c), c_q // wq + g)),
                  pl.BlockSpec((L, wq), lambda b, g, c: (rows(b, c), c_f // wq + g)),
                  pl.BlockSpec((L, wv), lambda b, g, c: (rows(b, c), c_i // wv + g)),
                  pl.BlockSpec((L, wv), lambda b, g, c: (rows(b, c), c_g // wv + g)),
                  pl.BlockSpec((1, HP, DV, DK), lambda b, g, c: (b, g, 0, 0)),
                  pl.BlockSpec((lb_logits.shape[0], wq), lambda b, g, c: (0, g)),
                  pl.BlockSpec((1, wv), lambda b, g, c: (0, g))],
        out_specs=[pl.BlockSpec((L, wv), lambda b, g, c: (b * nC + c, g)),
                   pl.BlockSpec((1, HP, DV, DK), lambda b, g, c: (b, g, 0, 0))],
        out_shape=[jax.ShapeDtypeStruct((B * T, NH * DV), BF16),
                   jax.ShapeDtypeStruct((B, NH, DV, DK), F32)],
        scratch_shapes=[pltpu.VMEM((HP, DV, DK), F32)],
        compiler_params=_cparams(("arbitrary", "arbitrary", "arbitrary")),
        name="mixer_hgrn2",
    )(p, p, p, p, s0t, lb_logits, hn)


def _gmm1_kernel(be_ref, na_ref, x_ref, w1_ref, w3_ref, o_ref):
    b = pl.program_id(0)

    @pl.when(b < na_ref[0])
    def _():
        x = x_ref[...]
        o_ref[...] = (_silu(_dot(x, w1_ref[0])) * _dot(x, w3_ref[0])).astype(o_ref.dtype)

    @pl.when(b >= na_ref[0])
    def _():
        o_ref[...] = jnp.zeros_like(o_ref)


def _gmm2_kernel(be_ref, na_ref, h_ref, w2_ref, rw_ref, o_ref):
    b = pl.program_id(0)

    @pl.when(b < na_ref[0])
    def _():
        o_ref[...] = _dot(h_ref[...], w2_ref[0]) * rw_ref[...]

    @pl.when(b >= na_ref[0])
    def _():
        o_ref[...] = jnp.zeros_like(o_ref)


def _expert_mlp(xg, roww, blk_e, n_act, w1, w3, w2):
    rows, d = xg.shape
    f = w1.shape[2]
    r = MOE_ROWS
    nb = rows // r

    def act(b, na):
        return jnp.minimum(b, na[0] - 1)

    hb = pl.pallas_call(
        _gmm1_kernel,
        grid_spec=pltpu.PrefetchScalarGridSpec(
            num_scalar_prefetch=2, grid=(nb,),
            in_specs=[pl.BlockSpec((r, d), lambda b, be, na: (act(b, na), 0)),
                      pl.BlockSpec((1, d, f), lambda b, be, na: (be[b], 0, 0)),
                      pl.BlockSpec((1, d, f), lambda b, be, na: (be[b], 0, 0))],
            out_specs=pl.BlockSpec((r, f), lambda b, be, na: (b, 0))),
        out_shape=jax.ShapeDtypeStruct((rows, f), BF16),
        compiler_params=_cparams(("arbitrary",)),
        name="expert_up",
    )(blk_e, n_act, xg, w1, w3)
    return pl.pallas_call(
        _gmm2_kernel,
        grid_spec=pltpu.PrefetchScalarGridSpec(
            num_scalar_prefetch=2, grid=(nb,),
            in_specs=[pl.BlockSpec((r, f), lambda b, be, na: (act(b, na), 0)),
                      pl.BlockSpec((1, f, d), lambda b, be, na: (be[b], 0, 0)),
                      pl.BlockSpec((r, 1), lambda b, be, na: (act(b, na), 0))],
            out_specs=pl.BlockSpec((r, d), lambda b, be, na: (b, 0))),
        out_shape=jax.ShapeDtypeStruct((rows, d), F32),
        compiler_params=_cparams(("arbitrary",)),
        name="expert_down",
    )(blk_e, n_act, hb, w2, roww)


def _route(logits):
    n = logits.shape[0]
    pg = jax.nn.softmax(logits[:, :N_GROUPS], axis=-1)
    grp = jnp.argmax(pg, axis=-1)
    p_grp = jnp.max(pg, axis=-1)
    le = logits[:, N_GROUPS:N_GROUPS + N_EXPERTS].reshape(n, N_GROUPS, EXPERTS_PER_GROUP)
    le_g = le[jnp.arange(n), grp]
    top_logit, top_j = lax.top_k(le_g, TOP_K)
    wts = jax.nn.softmax(top_logit, axis=-1) * p_grp[:, None]
    eid = (grp[:, None] * EXPERTS_PER_GROUP + top_j).astype(jnp.int32)
    return eid, wts


def _dispatch(eid, wts):
    n = eid.shape[0]
    a = n * TOP_K
    r = MOE_ROWS
    nb = -(-a // r) + N_EXPERTS
    flat_e = eid.reshape(a)
    order = jnp.argsort(flat_e)
    se = flat_e[order]
    counts = jnp.bincount(flat_e, length=N_EXPERTS)
    padded = (counts + r - 1) // r * r
    pend = jnp.cumsum(padded)
    pstart = pend - padded
    start = jnp.cumsum(counts) - counts
    dest = (pstart[se] + jnp.arange(a, dtype=jnp.int32) - start[se]).astype(jnp.int32)
    rows = jnp.full((nb * r,), n, dtype=jnp.int32).at[dest].set((order // TOP_K).astype(jnp.int32))
    roww = jnp.zeros((nb * r,), F32).at[dest].set(wts.reshape(a)[order])
    n_act = (pend[-1] // r).astype(jnp.int32)
    blk = jnp.arange(nb, dtype=jnp.int32)
    blk_e = jnp.minimum(jnp.searchsorted(pend, jnp.minimum(blk, n_act - 1) * r, side='right'),
                        N_EXPERTS - 1).astype(jnp.int32)
    slot = jnp.zeros((a,), jnp.int32).at[order].set(dest).reshape(n, TOP_K)
    return rows, roww.reshape(nb * r, 1), blk_e, n_act.reshape(1), slot


def kernel(x_prompt, x_sample, c_prompt, c_sample, state_a_C, state_a_n, state_a_m, state_a_conv, state_b_S, state_c_S, w_ada, b_ada, norm_mix, w_in, conv_w, conv_b, b_gate_a, hn_a, w_gk2, b_gk, hn_b, lb_logits, hn_c, w_br_a, w_br_b, w_br_c, w_out, norm_moe, w_rg, b_rg, w_re, b_re, w_exp1, w_exp3, w_exp2, norm_final):
    depth = w_ada.shape[0]
    bp, tp, d = x_prompt.shape
    bs, ts, _ = x_sample.shape
    nh_a, dv_a, dk_a = state_a_C.shape[2:]
    nh_b, dk_b, dv_b = state_b_S.shape[2:]
    nh_c, dk_c, dv_c = state_c_S.shape[2:]
    gate_rank = w_gk2.shape[1]
    qk_a, w_a = nh_a * dk_a, nh_a * dv_a
    qk_b, w_b = nh_b * dk_b, nh_b * dv_b
    qk_c, w_c = nh_c * dk_c, nh_c * dv_c
    mp, ms = bp * tp, bs * ts
    m = mp + ms
    assert bp == 1 and 1 + bs <= MOD_ROWS and m % TM_BIG == 0 and ms <= TM_SMALL

    sizes = (2 * qk_a, w_a, 2 * nh_a, w_a, qk_b, qk_b, w_b, gate_rank, w_b,
             qk_c, qk_c, w_c, w_c, N_BRANCH * d)
    names = ("a_qk", "a_v", "a_if", "a_o", "b_q", "b_k", "b_v", "b_lr", "b_g",
             "c_q", "c_f", "c_i", "c_g", "gates")
    src, o = {}, 0
    for nm, sz in zip(names, sizes):
        src[nm] = (o, sz)
        o += sz
    order = ("a_qk", "a_v", "a_o", "b_q", "b_k", "b_v", "b_g", "c_q", "c_f", "c_i", "c_g",
             "gates", "a_if", "b_lr")
    col, o = {}, 0
    for nm in order:
        col[nm] = o
        o += -(-src[nm][1] // LANES) * LANES
    n_cols = o
    assert n_cols % TN_IN == 0

    def pack_w_in(w):
        parts = []
        for nm in order:
            s0, sz = src[nm]
            wpart = w[:, s0:s0 + sz]
            pad = -sz % LANES
            if pad:
                wpart = jnp.pad(wpart, ((0, 0), (0, pad)))
            parts.append(wpart.astype(BF16))
        return jnp.concatenate(parts, axis=1)

    x = jnp.concatenate([x_prompt.reshape(mp, d), x_sample.reshape(ms, d)], axis=0)
    row_cond = jnp.concatenate([jnp.zeros((mp,), jnp.int32),
                                1 + jnp.arange(ms, dtype=jnp.int32) // ts])
    rsel = (row_cond[:, None] == jnp.arange(MOD_ROWS, dtype=jnp.int32)[None, :]).astype(BF16)
    c_all = jnp.concatenate([c_prompt, c_sample,
                             jnp.zeros((MOD_ROWS - bp - bs, d), F32)], axis=0)
    mod = _modulation(c_all, w_ada, b_ada)

    zeros = lambda *s: jnp.zeros(s, F32)
    pad_m = lambda mm: jnp.pad(mm, ((0, 0), (0, LANES - mm.shape[1])))[:, None, :]
    outs_p = [[] for _ in range(6)]
    outs_s = [[] for _ in range(6)]

    for l in range(depth):
        mod_l = mod[l:l + 1]
        h = _norm_mod(x, rsel, norm_mix[l], mod_l, 0, 1)
        p = _matmul(h, pack_w_in(w_in[l]), F32, TM_BIG, TN_IN)

        bias_if = jnp.pad(b_gate_a[l].reshape(1, 2 * nh_a), ((0, 0), (0, LANES - 2 * nh_a)))
        w_gk2p = jnp.pad(w_gk2[l], ((0, LANES - gate_rank), (0, 0)))
        groups = (
            dict(row0=0, B=bp, T=tp, La=256, Lg=64,
                 conv0=zeros(bp, CONV_W - 1, 2 * qk_a), c0=zeros(bp, nh_a, dv_a, dk_a),
                 n0=zeros(bp, nh_a, dk_a), m0=zeros(bp, 1, LANES),
                 sb0=zeros(bp, nh_b, dv_b, dk_b), sc0=zeros(bp, nh_c, dv_c, dk_c)),
            dict(row0=mp, B=bs, T=ts, La=ts, Lg=ts,
                 conv0=state_a_conv[l], c0=state_a_C[l], n0=state_a_n[l],
                 m0=pad_m(state_a_m[l]),
                 sb0=jnp.swapaxes(state_b_S[l], -1, -2), sc0=jnp.swapaxes(state_c_S[l], -1, -2)),
        )
        o_a, o_b, o_c = [], [], []
        for g, outs in zip(groups, (outs_p, outs_s)):
            oa, a_c, a_n, a_m, a_conv = _mlstm(
                p, (col["a_qk"], col["a_v"], col["a_o"], col["a_if"]),
                g["conv0"], g["c0"], g["n0"], g["m0"], conv_w[l], conv_b[l].reshape(1, -1),
                bias_if, hn_a[l].reshape(1, -1),
                row0=g["row0"], B=g["B"], T=g["T"], L=g["La"], NH=nh_a, DK=dk_a, DV=dv_a)
            ob, b_st = _gla_b(
                p, (col["b_q"], col["b_k"], col["b_v"], col["b_g"], col["b_lr"]),
                g["sb0"], w_gk2p, b_gk[l].reshape(1, -1), hn_b[l].reshape(1, -1),
                row0=g["row0"], B=g["B"], T=g["T"], L=g["Lg"], NH=nh_b, DK=dk_b, DV=dv_b)
            oc, c_st = _gla_c(
                p, (col["c_q"], col["c_f"], col["c_i"], col["c_g"]),
                g["sc0"], lb_logits, hn_c[l].reshape(1, -1),
                layer=l, row0=g["row0"], B=g["B"], T=g["T"], L=g["Lg"],
                NH=nh_c, DK=dk_c, DV=dv_c, HP=4)
            o_a.append(oa)
            o_b.append(ob)
            o_c.append(oc)
            for lst, val in zip(outs, (a_c, a_n, a_m[:, 0, :nh_a], a_conv,
                                       jnp.swapaxes(b_st, -1, -2), jnp.swapaxes(c_st, -1, -2))):
                lst.append(val)

        merged = _merge(jnp.concatenate(o_a), jnp.concatenate(o_b), jnp.concatenate(o_c),
                        w_br_a[l].astype(BF16), w_br_b[l].astype(BF16), w_br_c[l].astype(BF16),
                        p, col["gates"])
        x = _out_proj(merged, w_out[l].astype(BF16), x, rsel, mod_l, 2)

        w_r = jnp.pad(jnp.concatenate([w_rg[l], w_re[l]], axis=1),
                      ((0, 0), (0, LANES - N_GROUPS - N_EXPERTS)))
        b_r = jnp.pad(jnp.concatenate([b_rg[l], b_re[l]]),
                      (0, LANES - N_GROUPS - N_EXPERTS)).reshape(1, LANES)
        h2, logits = _norm_mod_router(x, rsel, norm_moe[l], mod_l, 3, 4, w_r, b_r)
        eid, wts = _route(logits)
        rows, roww, blk_e, n_act, slot = _dispatch(eid, wts)
        h2pad = jnp.concatenate([h2, jnp.zeros((1, d), BF16)], axis=0)
        yb = _expert_mlp(h2pad[rows], roww, blk_e, n_act,
                         w_exp1[l].astype(BF16), w_exp3[l].astype(BF16), w_exp2[l].astype(BF16))
        x = _gated_add(x, yb[slot[:, 0]], yb[slot[:, 1]], rsel, mod_l, 5)

    y_prompt = _final_norm(x, norm_final, 0, mp, 512).reshape(bp, tp, d)
    y_sample = _final_norm(x, norm_final, mp, ms, ms).reshape(bs, ts, d)
    stack = lambda lst: jnp.stack(lst)
    return (y_prompt, y_sample,
            stack(outs_p[0]), stack(outs_p[1]), stack(outs_p[2]), stack(outs_p[3]),
            stack(outs_p[4]), stack(outs_p[5]),
            stack(outs_s[0]), stack(outs_s[1]), stack(outs_s[2]), stack(outs_s[3]),
            stack(outs_s[4]), stack(outs_s[5]))
```

```python
import functools

import jax
import jax.numpy as jnp
import numpy as np
from jax import lax
from jax.experimental import pallas as pl
from jax.experimental.pallas import tpu as pltpu

F32 = jnp.float32
BF16 = jnp.bfloat16

NORM_EPS = 1e-6
NEG_BIG = -1e30
F_TINY = 1e-30
GLA_NORM = 16.0
CONV_W = 4
TOP_K = 2
N_GROUPS = 4
EXPERTS_PER_GROUP = 8
N_EXPERTS = N_GROUPS * EXPERTS_PER_GROUP
N_BRANCH = 3

LANES = 128
MOD_ROWS = 16
VMEM_LIMIT = 56 * 1024 * 1024

TM_BIG = 1040
TM_SMALL = 520
TN_IN = 768
MOE_ROWS = 256
SUB = 16
SAFE_LOG = 60.0


def _cparams(sem):
    return pltpu.CompilerParams(dimension_semantics=sem, vmem_limit_bytes=VMEM_LIMIT)


def _dot(a, b):
    return jnp.dot(a, b, preferred_element_type=F32)


def _dot_nt(a, b):
    return lax.dot_general(a, b, (((1,), (1,)), ((), ())), preferred_element_type=F32)


def _dot_tn(a, b):
    return lax.dot_general(a, b, (((0,), (0,)), ((), ())), preferred_element_type=F32)


def _split3(x):
    hi = x.astype(BF16)
    r = x - hi.astype(F32)
    mid = r.astype(BF16)
    lo = (r - mid.astype(F32)).astype(BF16)
    return hi, mid, lo


def _dot_sel(sel, x):
    hi, mid, lo = _split3(x)
    return _dot(sel, hi) + _dot(sel, mid) + _dot(sel, lo)


def _dot_nt_sel(sel, x):
    hi, mid, lo = _split3(x)
    return _dot_nt(sel, hi) + _dot_nt(sel, mid) + _dot_nt(sel, lo)


def _dot_nt_f32(a, b):
    a_hi = a.astype(BF16)
    a_lo = (a - a_hi.astype(F32)).astype(BF16)
    b_hi = b.astype(BF16)
    b_lo = (b - b_hi.astype(F32)).astype(BF16)
    return _dot_nt(a_hi, b_hi) + _dot_nt(a_hi, b_lo) + _dot_nt(a_lo, b_hi)


def _r16(x):
    return x.astype(BF16).astype(F32)


def _sigmoid(x):
    return jax.nn.sigmoid(x)


def _silu(x):
    return x * jax.nn.sigmoid(x)


def _log_sigmoid(x):
    return jnp.minimum(x, 0.0) - jnp.log1p(jnp.exp(-jnp.abs(x)))


def _rms(x):
    return x * lax.rsqrt(jnp.mean(x * x, axis=-1, keepdims=True) + NORM_EPS)


def _tril(n):
    r = lax.broadcasted_iota(jnp.int32, (n, n), 0)
    c = lax.broadcasted_iota(jnp.int32, (n, n), 1)
    return jnp.where(r >= c, 1.0, 0.0).astype(BF16)


def _eye(n):
    r = lax.broadcasted_iota(jnp.int32, (n, n), 0)
    c = lax.broadcasted_iota(jnp.int32, (n, n), 1)
    return jnp.where(r == c, 1.0, 0.0).astype(BF16)


def _mod_kernel(c_ref, w_ref, b_ref, o_ref):
    c = c_ref[...]
    o_ref[0] = _dot(_silu(c).astype(BF16), w_ref[0].astype(BF16)) + b_ref[0]


def _modulation(c_all, w_ada, b_ada):
    depth, d, n = w_ada.shape
    tn = 1024
    return pl.pallas_call(
        _mod_kernel,
        grid=(depth, n // tn),
        in_specs=[pl.BlockSpec((MOD_ROWS, d), lambda l, j: (0, 0)),
                  pl.BlockSpec((1, d, tn), lambda l, j: (l, 0, j)),
                  pl.BlockSpec((1, 1, tn), lambda l, j: (l, 0, j))],
        out_specs=pl.BlockSpec((1, MOD_ROWS, tn), lambda l, j: (l, 0, j)),
        out_shape=jax.ShapeDtypeStruct((depth, MOD_ROWS, n), F32),
        compiler_params=_cparams(("arbitrary", "arbitrary")),
        name="modulation",
    )(c_all, w_ada, b_ada.reshape(depth, 1, n))


def _row_mod(mixed, r_ref, m_ref):
    if mixed:
        return _dot_sel(r_ref[...], m_ref[0])
    return m_ref[0, 0:1, :]


def _norm_mod_kernel(x_ref, r_ref, g_ref, sh_ref, sc_ref, o_ref):
    last = pl.num_programs(0) - 1

    def body(mixed):
        y = _rms(x_ref[...]) * g_ref[...]
        o_ref[...] = (y * (1.0 + _row_mod(mixed, r_ref, sc_ref)) + _row_mod(mixed, r_ref, sh_ref)
                      ).astype(o_ref.dtype)

    pl.when(pl.program_id(0) != last)(lambda: body(False))
    pl.when(pl.program_id(0) == last)(lambda: body(True))


def _norm_mod(x, rsel, gain, mod_l, sh_idx, sc_idx):
    m, d = x.shape
    tm = TM_SMALL
    return pl.pallas_call(
        _norm_mod_kernel,
        grid=(m // tm,),
        in_specs=[pl.BlockSpec((tm, d), lambda i: (i, 0)),
                  pl.BlockSpec((tm, MOD_ROWS), lambda i: (i, 0)),
                  pl.BlockSpec((1, d), lambda i: (0, 0)),
                  pl.BlockSpec((1, MOD_ROWS, d), lambda i: (0, 0, sh_idx)),
                  pl.BlockSpec((1, MOD_ROWS, d), lambda i: (0, 0, sc_idx))],
        out_specs=pl.BlockSpec((tm, d), lambda i: (i, 0)),
        out_shape=jax.ShapeDtypeStruct((m, d), BF16),
        compiler_params=_cparams(("arbitrary",)),
        name="norm_mod",
    )(x, rsel, gain.reshape(1, d), mod_l, mod_l)


def _norm_mod_router_kernel(x_ref, r_ref, g_ref, sh_ref, sc_ref, wr_ref, br_ref, o_ref, lg_ref):
    last = pl.num_programs(0) - 1

    def body(mixed):
        y = _rms(x_ref[...]) * g_ref[...]
        h = y * (1.0 + _row_mod(mixed, r_ref, sc_ref)) + _row_mod(mixed, r_ref, sh_ref)
        hb = h.astype(o_ref.dtype)
        o_ref[...] = hb
        lg_ref[...] = _dot(hb, wr_ref[...].astype(BF16)) + br_ref[...]

    pl.when(pl.program_id(0) != last)(lambda: body(False))
    pl.when(pl.program_id(0) == last)(lambda: body(True))


def _norm_mod_router(x, rsel, gain, mod_l, sh_idx, sc_idx, w_r, b_r):
    m, d = x.shape
    tm = TM_SMALL
    return pl.pallas_call(
        _norm_mod_router_kernel,
        grid=(m // tm,),
        in_specs=[pl.BlockSpec((tm, d), lambda i: (i, 0)),
                  pl.BlockSpec((tm, MOD_ROWS), lambda i: (i, 0)),
                  pl.BlockSpec((1, d), lambda i: (0, 0)),
                  pl.BlockSpec((1, MOD_ROWS, d), lambda i: (0, 0, sh_idx)),
                  pl.BlockSpec((1, MOD_ROWS, d), lambda i: (0, 0, sc_idx)),
                  pl.BlockSpec((d, LANES), lambda i: (0, 0)),
                  pl.BlockSpec((1, LANES), lambda i: (0, 0))],
        out_specs=[pl.BlockSpec((tm, d), lambda i: (i, 0)),
                   pl.BlockSpec((tm, LANES), lambda i: (i, 0))],
        out_shape=[jax.ShapeDtypeStruct((m, d), BF16),
                   jax.ShapeDtypeStruct((m, LANES), F32)],
        compiler_params=_cparams(("arbitrary",)),
        name="norm_mod_router",
    )(x, rsel, gain.reshape(1, d), mod_l, mod_l, w_r, b_r)


def _final_norm_kernel(x_ref, g_ref, o_ref):
    o_ref[...] = _rms(x_ref[...]) * g_ref[...]


def _final_norm(x, gain, row0, rows, tm):
    d = x.shape[1]
    blk0 = row0 // tm
    return pl.pallas_call(
        _final_norm_kernel,
        grid=(rows // tm,),
        in_specs=[pl.BlockSpec((tm, d), lambda i: (blk0 + i, 0)),
                  pl.BlockSpec((1, d), lambda i: (0, 0))],
        out_specs=pl.BlockSpec((tm, d), lambda i: (i, 0)),
        out_shape=jax.ShapeDtypeStruct((rows, d), F32),
        compiler_params=_cparams(("arbitrary",)),
        name="final_norm",
    )(x, gain.reshape(1, d))


def _matmul_kernel(a_ref, b_ref, o_ref):
    o_ref[...] = _dot(a_ref[...], b_ref[...]).astype(o_ref.dtype)


def _matmul(a, b, out_dtype, tm, tn):
    m, k = a.shape
    n = b.shape[1]
    return pl.pallas_call(
        _matmul_kernel,
        grid=(m // tm, n // tn),
        in_specs=[pl.BlockSpec((tm, k), lambda i, j: (i, 0)),
                  pl.BlockSpec((k, tn), lambda i, j: (0, j))],
        out_specs=pl.BlockSpec((tm, tn), lambda i, j: (i, j)),
        out_shape=jax.ShapeDtypeStruct((m, n), out_dtype),
        compiler_params=_cparams(("arbitrary", "arbitrary")),
        name="in_proj",
    )(a, b)


def _merge_kernel(oa_ref, ob_ref, oc_ref, wa_ref, wb_ref, wc_ref, ga_ref, gb_ref, gc_ref, o_ref):
    acc = _sigmoid(ga_ref[...]) * _dot(oa_ref[...], wa_ref[...])
    acc = acc + _sigmoid(gb_ref[...]) * _dot(ob_ref[...], wb_ref[...])
    acc = acc + _sigmoid(gc_ref[...]) * _dot(oc_ref[...], wc_ref[...])
    o_ref[...] = acc.astype(o_ref.dtype)


def _merge(o_a, o_b, o_c, w_a, w_b, w_c, layer, p, gate_col0):
    m, kw = o_a.shape
    d = w_a.shape[2]
    tm, tn = TM_SMALL, 512
    g0 = gate_col0 // tn
    gstride = d // tn
    o_spec = pl.BlockSpec((tm, kw), lambda j, i: (i, 0))
    w_spec = pl.BlockSpec((None, kw, tn), lambda j, i: (layer, 0, j))

    def g_spec(br):
        return pl.BlockSpec((tm, tn), lambda j, i: (i, g0 + br * gstride + j))

    return pl.pallas_call(
        _merge_kernel,
        grid=(d // tn, m // tm),
        in_specs=[o_spec, o_spec, o_spec, w_spec, w_spec, w_spec, g_spec(0), g_spec(1), g_spec(2)],
        out_specs=pl.BlockSpec((tm, tn), lambda j, i: (i, j)),
        out_shape=jax.ShapeDtypeStruct((m, d), BF16),
        compiler_params=_cparams(("arbitrary", "arbitrary")),
        name="merge",
    )(o_a, o_b, o_c, w_a, w_b, w_c, p, p, p)


def _out_proj_kernel(a_ref, w_ref, x_ref, r_ref, g_ref, o_ref):
    last = pl.num_programs(0) - 1
    acc = _dot(a_ref[...], w_ref[...])

    def body(mixed):
        o_ref[...] = x_ref[...] + _row_mod(mixed, r_ref, g_ref) * acc

    pl.when(pl.program_id(0) != last)(lambda: body(False))
    pl.when(pl.program_id(0) == last)(lambda: body(True))


def _out_proj(a, w, layer, x, rsel, mod_l, g_idx):
    m, k = a.shape
    d = w.shape[2]
    tm, tn = TM_BIG, 512
    nj = d // tn
    return pl.pallas_call(
        _out_proj_kernel,
        grid=(m // tm, nj),
        in_specs=[pl.BlockSpec((tm, k), lambda i, j: (i, 0)),
                  pl.BlockSpec((None, k, tn), lambda i, j: (layer, 0, j)),
                  pl.BlockSpec((tm, tn), lambda i, j: (i, j)),
                  pl.BlockSpec((tm, MOD_ROWS), lambda i, j: (i, 0)),
                  pl.BlockSpec((1, MOD_ROWS, tn), lambda i, j: (0, 0, g_idx * nj + j))],
        out_specs=pl.BlockSpec((tm, tn), lambda i, j: (i, j)),
        out_shape=jax.ShapeDtypeStruct((m, d), F32),
        compiler_params=_cparams(("arbitrary", "arbitrary")),
        name="out_proj",
    )(a, w, x, rsel, mod_l)


def _gated_add_kernel(x_ref, y0_ref, y1_ref, r_ref, g_ref, o_ref):
    last = pl.num_programs(0) - 1

    def body(mixed):
        o_ref[...] = x_ref[...] + _row_mod(mixed, r_ref, g_ref) * (y0_ref[...] + y1_ref[...])

    pl.when(pl.program_id(0) != last)(lambda: body(False))
    pl.when(pl.program_id(0) == last)(lambda: body(True))


def _gated_add(x, y0, y1, rsel, mod_l, g_idx):
    m, d = x.shape
    tm, tn = TM_SMALL, 1024
    nj = d // tn
    spec = pl.BlockSpec((tm, tn), lambda i, j: (i, j))
    return pl.pallas_call(
        _gated_add_kernel,
        grid=(m // tm, nj),
        in_specs=[spec, spec, spec,
                  pl.BlockSpec((tm, MOD_ROWS), lambda i, j: (i, 0)),
                  pl.BlockSpec((1, MOD_ROWS, tn), lambda i, j: (0, 0, g_idx * nj + j))],
        out_specs=spec,
        out_shape=jax.ShapeDtypeStruct((m, d), F32),
        compiler_params=_cparams(("arbitrary", "arbitrary")),
        name="moe_combine",
    )(x, y0, y1, rsel, mod_l)


def _alias_prev(kern, in_specs, args, o_prev):
    if o_prev is None:
        return kern, in_specs, args, {}
    n = len(args)

    def wrapped(*refs):
        return kern(*refs[:n], *refs[n + 1:])

    return wrapped, in_specs + [pl.BlockSpec(memory_space=pl.ANY)], args + [o_prev], {n: 0}


def _mlstm_kernel(qp_ref, kp_ref, v_ref, og_ref, if_ref, conv0_ref, c0_ref, n0_ref, m0_ref,
                  cw_ref, cb_ref, bif_ref, hn_ref,
                  o_ref, cout_ref, nout_ref, mout_ref, convout_ref,
                  c_s, n_s, m_s, ubuf, q_s, k_s, *, L, NH, DK, DV):
    ci = pl.program_id(1)
    QK = NH * DK

    @pl.when(ci == 0)
    def _():
        c_s[...] = c0_ref[0]
        n_s[...] = n0_ref[0]
        m_s[...] = m0_ref[0]
        ubuf[8 - (CONV_W - 1):8, :] = conv0_ref[0]

    ubuf[8:8 + L, 0:QK] = qp_ref[...]
    ubuf[8:8 + L, QK:2 * QK] = kp_ref[...]
    y = cb_ref[...]
    for j in range(CONV_W):
        y = y + ubuf[8 - (CONV_W - 1) + j:8 - (CONV_W - 1) + j + L, :] * cw_ref[j:j + 1, :]
    ubuf[0:8, :] = ubuf[L:L + 8, :]
    qk = _silu(y)
    q_s[...] = qk[:, 0:QK] * (DK ** -0.5)
    k_s[...] = qk[:, QK:2 * QK]

    ifv = if_ref[...] + bif_ref[...]
    b_all = _dot_sel(_tril(L), _log_sigmoid(ifv))
    eye = _eye(LANES)
    ig_t = _dot_nt_sel(eye, ifv)
    b_t = _dot_nt_sel(eye, b_all)
    row = lax.broadcasted_iota(jnp.int32, (L, L), 0)
    col = lax.broadcasted_iota(jnp.int32, (L, L), 1)
    causal = row >= col
    lane = lax.broadcasted_iota(jnp.int32, (1, LANES), 1)
    m_old = m_s[...]
    m_new = m_old

    for h in range(NH):
        b_c = b_all[:, NH + h:NH + h + 1]
        ig_c = ifv[:, h:h + 1]
        b_r = b_t[NH + h:NH + h + 1, :]
        ig_r = ig_t[h:h + 1, :]
        m_prev = m_old[:, h:h + 1]
        log_d = jnp.where(causal, b_c - b_r + ig_r, NEG_BIG)
        inter = b_c + m_prev
        mt = jnp.maximum(jnp.max(log_d, axis=-1, keepdims=True), inter)
        dm = jnp.exp(log_d - mt)
        sc = jnp.exp(inter - mt)
        qh = q_s[:, h * DK:(h + 1) * DK]
        kh = k_s[:, h * DK:(h + 1) * DK]
        vh = v_ref[:, h * DV:(h + 1) * DV]
        qb, kb, vb = qh.astype(BF16), kh.astype(BF16), vh.astype(BF16)
        c_h = c_s[h]
        n_h = n_s[h:h + 1, :]
        s = _dot_nt(qb, kb) * dm
        num = sc * _dot_nt(qb, c_h.astype(BF16)) + _dot(s.astype(BF16), vb)
        den = (sc * jnp.sum(qb.astype(F32) * _r16(n_h), axis=-1, keepdims=True)
               + jnp.sum(s, axis=-1, keepdims=True))
        hc = num / jnp.maximum(jnp.abs(den), jnp.exp(-mt))
        mt_l = mt[L - 1:L, :]
        w_l = jnp.exp((b_c[L - 1:L, :] - b_c) + ig_c - mt_l)
        s_l = sc[L - 1:L, :]
        c_s[h] = s_l * c_h + _dot_tn(vb, (kh * w_l).astype(BF16))
        n_s[h:h + 1, :] = s_l * n_h + jnp.sum(_r16(w_l) * kb.astype(F32), axis=0, keepdims=True)
        m_new = jnp.where(lane == h, mt_l, m_new)
        out = _rms(hc) * hn_ref[:, h * DV:(h + 1) * DV] * _sigmoid(og_ref[:, h * DV:(h + 1) * DV])
        o_ref[:, h * DV:(h + 1) * DV] = out.astype(o_ref.dtype)

    m_s[...] = m_new

    @pl.when(ci == pl.num_programs(1) - 1)
    def _():
        cout_ref[0] = c_s[...]
        nout_ref[0] = n_s[...]
        mout_ref[0] = m_s[...]
        convout_ref[0] = ubuf[8 - (CONV_W - 1):8, :]


def _mlstm(p, cols, conv0, c0, n0, m0, conv_w, conv_b, bias_if, hn, o_prev,
           *, row0, B, T, L, NH, DK, DV):
    QK, W = NH * DK, NH * DV
    nC = T // L
    rb0 = row0 // L
    c_qk, c_v, c_o, c_if = cols

    def rows(b, c):
        return rb0 + b * nC + c

    kern = functools.partial(_mlstm_kernel, L=L, NH=NH, DK=DK, DV=DV)
    in_specs = [pl.BlockSpec((L, QK), lambda b, c: (rows(b, c), c_qk // QK)),
                  pl.BlockSpec((L, QK), lambda b, c: (rows(b, c), c_qk // QK + 1)),
                  pl.BlockSpec((L, W), lambda b, c: (rows(b, c), c_v // W)),
                  pl.BlockSpec((L, W), lambda b, c: (rows(b, c), c_o // W)),
                  pl.BlockSpec((L, LANES), lambda b, c: (rows(b, c), c_if // LANES)),
                  pl.BlockSpec((1, CONV_W - 1, 2 * QK), lambda b, c: (b, 0, 0)),
                  pl.BlockSpec((1, NH, DV, DK), lambda b, c: (b, 0, 0, 0)),
                  pl.BlockSpec((1, NH, DK), lambda b, c: (b, 0, 0)),
                  pl.BlockSpec((1, 1, LANES), lambda b, c: (b, 0, 0)),
                  pl.BlockSpec((CONV_W, 2 * QK), lambda b, c: (0, 0)),
                  pl.BlockSpec((1, 2 * QK), lambda b, c: (0, 0)),
                  pl.BlockSpec((1, LANES), lambda b, c: (0, 0)),
                  pl.BlockSpec((1, W), lambda b, c: (0, 0))]
    args = [p, p, p, p, p, conv0, c0, n0, m0, conv_w, conv_b, bias_if, hn]
    kern, in_specs, args, aliases = _alias_prev(kern, in_specs, args, o_prev)
    return pl.pallas_call(
        kern,
        grid=(B, nC),
        in_specs=in_specs,
        out_specs=[pl.BlockSpec((L, W), lambda b, c: (rows(b, c), 0)),
                   pl.BlockSpec((1, NH, DV, DK), lambda b, c: (b, 0, 0, 0)),
                   pl.BlockSpec((1, NH, DK), lambda b, c: (b, 0, 0)),
                   pl.BlockSpec((1, 1, LANES), lambda b, c: (b, 0, 0)),
                   pl.BlockSpec((1, CONV_W - 1, 2 * QK), lambda b, c: (b, 0, 0))],
        out_shape=[jax.ShapeDtypeStruct((p.shape[0], W), BF16),
                   jax.ShapeDtypeStruct((B, NH, DV, DK), F32),
                   jax.ShapeDtypeStruct((B, NH, DK), F32),
                   jax.ShapeDtypeStruct((B, 1, LANES), F32),
                   jax.ShapeDtypeStruct((B, CONV_W - 1, 2 * QK), F32)],
        scratch_shapes=[pltpu.VMEM((NH, DV, DK), F32),
                        pltpu.VMEM((NH, DK), F32),
                        pltpu.VMEM((1, LANES), F32),
                        pltpu.VMEM((L + 8, 2 * QK), F32),
                        pltpu.VMEM((L, QK), F32),
                        pltpu.VMEM((L, QK), F32)],
        input_output_aliases=aliases,
        compiler_params=_cparams(("arbitrary", "arbitrary")),
        name="mixer_mlstm",
    )(*args)


def _gla_intra_exact(q, k, b, L, c):
    dk = q.shape[1]
    nsub = L // c
    q3 = q.reshape(nsub, c, dk)
    k3 = k.reshape(nsub, c, dk)
    b3 = b.reshape(nsub, c, dk)
    t_idx = lax.broadcasted_iota(jnp.int32, (1, c, 1), 1)
    s_idx = lax.broadcasted_iota(jnp.int32, (1, 1, c), 2)
    a_diag = jnp.zeros((nsub, c, c), F32)
    for s in range(c):
        arg = jnp.where(t_idx >= s, b3 - b3[:, s:s + 1, :], NEG_BIG)
        col_s = jnp.sum(q3 * k3[:, s:s + 1, :] * jnp.exp(arg), axis=-1, keepdims=True)
        a_diag = jnp.where(s_idx == s, col_s, a_diag)
    a_diag = a_diag.reshape(L, c)
    if nsub == 1:
        return a_diag
    row = lax.broadcasted_iota(jnp.int32, (L, L), 0)
    col = lax.broadcasted_iota(jnp.int32, (L, L), 1)
    rep_r = lax.broadcasted_iota(jnp.int32, (c, L), 0)
    rep_c = lax.broadcasted_iota(jnp.int32, (c, L), 1)
    rep = jnp.where((rep_c & (c - 1)) == rep_r, 1.0, 0.0).astype(BF16)
    a = jnp.where((row & -c) == (col & -c), _dot(a_diag.astype(BF16), rep), 0.0)
    blocks = [jnp.zeros((c, L), F32)]
    for i in range(1, nsub):
        r_i = b[i * c - 1:i * c, :]
        q_i = q[i * c:(i + 1) * c, :] * jnp.exp(b[i * c:(i + 1) * c, :] - r_i)
        k_i = k * jnp.exp(jnp.minimum(r_i - b, 0.0))
        blocks.append(_dot_nt_f32(q_i, k_i))
    return a + jnp.where((col & -c) < (row & -c), jnp.concatenate(blocks, axis=0), 0.0)


def _gla_intra_fact(q, k, b, L, c):
    nsub = L // c
    row = lax.broadcasted_iota(jnp.int32, (L, L), 0)
    col = lax.broadcasted_iota(jnp.int32, (L, L), 1)
    blocks = []
    for i in range(nsub):
        b_i = b[i * c:(i + 1) * c, :]
        if i == 0:
            q_i = q[0:c, :] * jnp.exp(b_i)
            k_i = k * jnp.exp(jnp.minimum(-b, SAFE_LOG))
        else:
            r_i = b[i * c - 1:i * c, :]
            q_i = q[i * c:(i + 1) * c, :] * jnp.exp(b_i - r_i)
            k_i = k * jnp.exp(jnp.minimum(r_i - b, SAFE_LOG))
        blocks.append(_dot_nt_f32(q_i, k_i))
    a = blocks[0] if nsub == 1 else jnp.concatenate(blocks, axis=0)
    return jnp.where(col <= row, a, 0.0)


def _gla_cum_decay(g, L):
    c = min(SUB, L)
    nsub = L // c
    b = _dot_sel(_tril(L), g)
    drops = [b[c - 1:c, :]] + [b[(i + 1) * c - 1:(i + 1) * c, :] - b[i * c - 1:i * c, :]
                               for i in range(1, nsub)]
    return b, jnp.min(drops[0] if nsub == 1 else jnp.concatenate(drops, axis=0))


def _either(pred, body):
    pl.when(pred)(lambda: body(True))
    pl.when(jnp.logical_not(pred))(lambda: body(False))


def _gla_chunk(q, k, v, b, st, L, fact):
    c = min(SUB, L)
    b_l = b[L - 1:L, :]
    o = _dot_nt((q * jnp.exp(b)).astype(BF16), st.astype(BF16))
    a = _gla_intra_fact(q, k, b, L, c) if fact else _gla_intra_exact(q, k, b, L, c)
    vb = v.astype(BF16)
    o = o + _dot(a.astype(BF16), vb)
    st_new = st * jnp.exp(b_l) + _dot_tn(vb, (k * jnp.exp(b_l - b)).astype(BF16))
    return o, st_new


def _gla_b_kernel(q_ref, k_ref, v_ref, gt_ref, lr_ref, s0_ref, w2_ref, bgk_ref, hn_ref,
                  o_ref, sout_ref, st_s, *, L, DK):
    ci = pl.program_id(2)

    @pl.when(ci == 0)
    def _():
        st_s[...] = s0_ref[0, 0]

    lg = _log_sigmoid(_dot(lr_ref[...].astype(BF16), w2_ref[...].astype(BF16)) + bgk_ref[...]) / GLA_NORM
    b, worst = _gla_cum_decay(lg, L)

    def body(fact):
        o, st_new = _gla_chunk(q_ref[...] * (DK ** -0.5), k_ref[...], v_ref[...], b, st_s[...],
                               L, fact)
        st_s[...] = st_new
        o_ref[...] = (_rms(o) * hn_ref[...] * _silu(gt_ref[...])).astype(o_ref.dtype)

    _either(worst > -SAFE_LOG, body)

    @pl.when(ci == pl.num_programs(2) - 1)
    def _():
        sout_ref[0, 0] = st_s[...]


def _gla_b(p, cols, s0t, w_gk2p, b_gk, hn, o_prev, *, row0, B, T, L, NH, DK, DV):
    nC = T // L
    rb0 = row0 // L
    c_q, c_k, c_v, c_g, c_lr = cols

    def rows(b, c):
        return rb0 + b * nC + c

    kern = functools.partial(_gla_b_kernel, L=L, DK=DK)
    in_specs = [pl.BlockSpec((L, DK), lambda b, h, c: (rows(b, c), c_q // DK + h)),
                pl.BlockSpec((L, DK), lambda b, h, c: (rows(b, c), c_k // DK + h)),
                pl.BlockSpec((L, DV), lambda b, h, c: (rows(b, c), c_v // DV + h)),
                pl.BlockSpec((L, DV), lambda b, h, c: (rows(b, c), c_g // DV + h)),
                pl.BlockSpec((L, LANES), lambda b, h, c: (rows(b, c), c_lr // LANES)),
                pl.BlockSpec((1, 1, DV, DK), lambda b, h, c: (b, h, 0, 0)),
                pl.BlockSpec((LANES, DK), lambda b, h, c: (0, h)),
                pl.BlockSpec((1, DK), lambda b, h, c: (0, h)),
                pl.BlockSpec((1, DV), lambda b, h, c: (0, h))]
    args = [p, p, p, p, p, s0t, w_gk2p, b_gk, hn]
    kern, in_specs, args, aliases = _alias_prev(kern, in_specs, args, o_prev)
    return pl.pallas_call(
        kern,
        grid=(B, NH, nC),
        in_specs=in_specs,
        out_specs=[pl.BlockSpec((L, DV), lambda b, h, c: (rows(b, c), h)),
                   pl.BlockSpec((1, 1, DV, DK), lambda b, h, c: (b, h, 0, 0))],
        out_shape=[jax.ShapeDtypeStruct((p.shape[0], NH * DV), BF16),
                   jax.ShapeDtypeStruct((B, NH, DV, DK), F32)],
        scratch_shapes=[pltpu.VMEM((DV, DK), F32)],
        input_output_aliases=aliases,
        compiler_params=_cparams(("arbitrary", "arbitrary", "arbitrary")),
        name="mixer_gla",
    )(*args)


def _gla_c_kernel(q_ref, f_ref, i_ref, gt_ref, s0_ref, lbl_ref, hn_ref,
                  o_ref, sout_ref, st_s, *, L, HP, DK, DV, layer):
    ci = pl.program_id(2)

    @pl.when(ci == 0)
    def _():
        st_s[...] = s0_ref[0]

    lbl = lbl_ref[...]
    e = jnp.exp(lbl - jnp.max(lbl, axis=0, keepdims=True))
    sm = e / jnp.sum(e, axis=0, keepdims=True)
    lb = jnp.sum(sm[0:layer + 1, :], axis=0, keepdims=True) - sm[0:1, :]

    f = lb + (1.0 - lb) * _sigmoid(f_ref[...])
    b, worst = _gla_cum_decay(jnp.log(jnp.maximum(f, F_TINY)), L)

    def body(fact):
        for h in range(HP):
            sl = slice(h * DK, (h + 1) * DK)
            sv = slice(h * DV, (h + 1) * DV)
            o, st_new = _gla_chunk(_silu(q_ref[:, sl]), 1.0 - f[:, sl], i_ref[:, sv], b[:, sl],
                                   st_s[h], L, fact)
            st_s[h] = st_new
            o_ref[:, sv] = (_rms(o) * hn_ref[:, sv] * _silu(gt_ref[:, sv])).astype(o_ref.dtype)

    _either(worst > -SAFE_LOG, body)

    @pl.when(ci == pl.num_programs(2) - 1)
    def _():
        sout_ref[0] = st_s[...]


def _gla_c(p, cols, s0t, lb_logits, hn, o_prev, *, layer, row0, B, T, L, NH, DK, DV, HP):
    nC = T // L
    rb0 = row0 // L
    c_q, c_f, c_i, c_g = cols
    wq, wv = HP * DK, HP * DV

    def rows(b, c):
        return rb0 + b * nC + c

    kern = functools.partial(_gla_c_kernel, L=L, HP=HP, DK=DK, DV=DV, layer=layer)
    in_specs = [pl.BlockSpec((L, wq), lambda b, g, c: (rows(b, c), c_q // wq + g)),
                pl.BlockSpec((L, wq), lambda b, g, c: (rows(b, c), c_f // wq + g)),
                pl.BlockSpec((L, wv), lambda b, g, c: (rows(b, c), c_i // wv + g)),
                pl.BlockSpec((L, wv), lambda b, g, c: (rows(b, c), c_g // wv + g)),
                pl.BlockSpec((1, HP, DV, DK), lambda b, g, c: (b, g, 0, 0)),
                pl.BlockSpec((lb_logits.shape[0], wq), lambda b, g, c: (0, g)),
                pl.BlockSpec((1, wv), lambda b, g, c: (0, g))]
    args = [p, p, p, p, s0t, lb_logits, hn]
    kern, in_specs, args, aliases = _alias_prev(kern, in_specs, args, o_prev)
    return pl.pallas_call(
        kern,
        grid=(B, NH // HP, nC),
        in_specs=in_specs,
        out_specs=[pl.BlockSpec((L, wv), lambda b, g, c: (rows(b, c), g)),
                   pl.BlockSpec((1, HP, DV, DK), lambda b, g, c: (b, g, 0, 0))],
        out_shape=[jax.ShapeDtypeStruct((p.shape[0], NH * DV), BF16),
                   jax.ShapeDtypeStruct((B, NH, DV, DK), F32)],
        scratch_shapes=[pltpu.VMEM((HP, DV, DK), F32)],
        input_output_aliases=aliases,
        compiler_params=_cparams(("arbitrary", "arbitrary", "arbitrary")),
        name="mixer_hgrn2",
    )(*args)


def _new_expert(be_ref):
    b = pl.program_id(1)
    return (b == 0) | (be_ref[b] != be_ref[jnp.maximum(b - 1, 0)])


def _gmm1_kernel(be_ref, na_ref, x_ref, w1_ref, w3_ref, o_ref, w1_s, w3_s):
    @pl.when(_new_expert(be_ref))
    def _():
        w1_s[...] = w1_ref[...].astype(BF16)
        w3_s[...] = w3_ref[...].astype(BF16)

    @pl.when(pl.program_id(1) < na_ref[0])
    def _():
        x = x_ref[...]
        o_ref[...] = (_silu(_dot(x, w1_s[...])) * _dot(x, w3_s[...])).astype(o_ref.dtype)

    @pl.when(pl.program_id(1) >= na_ref[0])
    def _():
        o_ref[...] = jnp.zeros_like(o_ref)


def _gmm2_kernel(be_ref, na_ref, h_ref, w2_ref, rw_ref, o_ref, w2_s):
    @pl.when(_new_expert(be_ref))
    def _():
        w2_s[...] = w2_ref[...].astype(BF16)

    @pl.when(pl.program_id(1) < na_ref[0])
    def _():
        o_ref[...] = _dot(h_ref[...], w2_s[...]) * rw_ref[...]

    @pl.when(pl.program_id(1) >= na_ref[0])
    def _():
        o_ref[...] = jnp.zeros_like(o_ref)


def _expert_mlp(xg, roww, blk_e, n_act, w1, w3, w2, layer):
    rows, d = xg.shape
    f = w1.shape[3]
    r = MOE_ROWS
    nb = rows // r
    tf, td = f // 2, d // 2

    def act(b, na):
        return jnp.minimum(b, na[0] - 1)

    hb = pl.pallas_call(
        _gmm1_kernel,
        grid_spec=pltpu.PrefetchScalarGridSpec(
            num_scalar_prefetch=2, grid=(f // tf, nb),
            in_specs=[pl.BlockSpec((r, d), lambda j, b, be, na: (act(b, na), 0)),
                      pl.BlockSpec((None, None, d, tf), lambda j, b, be, na: (layer, be[b], 0, j)),
                      pl.BlockSpec((None, None, d, tf), lambda j, b, be, na: (layer, be[b], 0, j))],
            out_specs=pl.BlockSpec((r, tf), lambda j, b, be, na: (b, j)),
            scratch_shapes=[pltpu.VMEM((d, tf), BF16), pltpu.VMEM((d, tf), BF16)]),
        out_shape=jax.ShapeDtypeStruct((rows, f), BF16),
        compiler_params=_cparams(("arbitrary", "arbitrary")),
        name="expert_up",
    )(blk_e, n_act, xg, w1, w3)
    return pl.pallas_call(
        _gmm2_kernel,
        grid_spec=pltpu.PrefetchScalarGridSpec(
            num_scalar_prefetch=2, grid=(d // td, nb),
            in_specs=[pl.BlockSpec((r, f), lambda j, b, be, na: (act(b, na), 0)),
                      pl.BlockSpec((None, None, f, td), lambda j, b, be, na: (layer, be[b], 0, j)),
                      pl.BlockSpec((r, 1), lambda j, b, be, na: (act(b, na), 0))],
            out_specs=pl.BlockSpec((r, td), lambda j, b, be, na: (b, j)),
            scratch_shapes=[pltpu.VMEM((f, td), BF16)]),
        out_shape=jax.ShapeDtypeStruct((rows, d), F32),
        compiler_params=_cparams(("arbitrary", "arbitrary")),
        name="expert_down",
    )(blk_e, n_act, hb, w2, roww)


def _route(logits):
    n = logits.shape[0]
    pg = jax.nn.softmax(logits[:, :N_GROUPS], axis=-1)
    grp = jnp.argmax(pg, axis=-1)
    p_grp = jnp.max(pg, axis=-1)
    le = logits[:, N_GROUPS:N_GROUPS + N_EXPERTS].reshape(n, N_GROUPS, EXPERTS_PER_GROUP)
    le_g = le[jnp.arange(n), grp]
    top_logit, top_j = lax.top_k(le_g, TOP_K)
    wts = jax.nn.softmax(top_logit, axis=-1) * p_grp[:, None]
    eid = (grp[:, None] * EXPERTS_PER_GROUP + top_j).astype(jnp.int32)
    return eid, wts


def _dispatch(eid, wts):
    n = eid.shape[0]
    a = n * TOP_K
    r = MOE_ROWS
    nb = -(-a // r) + N_EXPERTS
    flat_e = eid.reshape(a)
    order = jnp.argsort(flat_e)
    se = flat_e[order]
    counts = jnp.bincount(flat_e, length=N_EXPERTS)
    padded = (counts + r - 1) // r * r
    pend = jnp.cumsum(padded)
    pstart = pend - padded
    start = jnp.cumsum(counts) - counts
    dest = (pstart[se] + jnp.arange(a, dtype=jnp.int32) - start[se]).astype(jnp.int32)
    rows = jnp.zeros((nb * r,), jnp.int32).at[dest].set((order // TOP_K).astype(jnp.int32))
    roww = jnp.zeros((nb * r,), F32).at[dest].set(wts.reshape(a)[order])
    n_act = (pend[-1] // r).astype(jnp.int32)
    blk = jnp.arange(nb, dtype=jnp.int32)
    blk_e = jnp.minimum(jnp.searchsorted(pend, jnp.minimum(blk, n_act - 1) * r, side='right'),
                        N_EXPERTS - 1).astype(jnp.int32)
    slot = jnp.zeros((a,), jnp.int32).at[order].set(dest).reshape(n, TOP_K)
    return rows, roww.reshape(nb * r, 1), blk_e, n_act.reshape(1), slot


def kernel(x_prompt, x_sample, c_prompt, c_sample, state_a_C, state_a_n, state_a_m, state_a_conv, state_b_S, state_c_S, w_ada, b_ada, norm_mix, w_in, conv_w, conv_b, b_gate_a, hn_a, w_gk2, b_gk, hn_b, lb_logits, hn_c, w_br_a, w_br_b, w_br_c, w_out, norm_moe, w_rg, b_rg, w_re, b_re, w_exp1, w_exp3, w_exp2, norm_final):
    depth = w_ada.shape[0]
    bp, tp, d = x_prompt.shape
    bs, ts, _ = x_sample.shape
    nh_a, dv_a, dk_a = state_a_C.shape[2:]
    nh_b, dk_b, dv_b = state_b_S.shape[2:]
    nh_c, dk_c, dv_c = state_c_S.shape[2:]
    gate_rank = w_gk2.shape[1]
    qk_a, w_a = nh_a * dk_a, nh_a * dv_a
    qk_b, w_b = nh_b * dk_b, nh_b * dv_b
    qk_c, w_c = nh_c * dk_c, nh_c * dv_c
    mp, ms = bp * tp, bs * ts
    m = mp + ms
    assert bp == 1 and 1 + bs <= MOD_ROWS and m % TM_BIG == 0 and ms <= TM_SMALL

    sizes = (2 * qk_a, w_a, 2 * nh_a, w_a, qk_b, qk_b, w_b, gate_rank, w_b,
             qk_c, qk_c, w_c, w_c, N_BRANCH * d)
    names = ("a_qk", "a_v", "a_if", "a_o", "b_q", "b_k", "b_v", "b_lr", "b_g",
             "c_q", "c_f", "c_i", "c_g", "gates")
    src, o = {}, 0
    for nm, sz in zip(names, sizes):
        src[nm] = o
        o += sz
    n_src = o
    lr_lane = src["b_lr"] % LANES
    pieces = ((src["a_qk"], src["a_if"]), (src["a_o"], src["b_lr"]), (src["b_g"], n_src),
              (src["a_if"], src["a_if"] + LANES),
              (src["b_lr"] - lr_lane, src["b_lr"] - lr_lane + LANES))
    assert src["a_if"] % LANES == 0 and lr_lane + gate_rank <= LANES
    col, o = {}, 0
    for (s0, s1), members in zip(pieces, (("a_qk", "a_v"), ("a_o", "b_q", "b_k", "b_v"),
                                          ("b_g", "c_q", "c_f", "c_i", "c_g", "gates"),
                                          ("a_if",), ())):
        assert (s1 - s0) % LANES == 0
        for nm in members:
            col[nm] = o + src[nm] - s0
        o += s1 - s0
    col["b_lr"] = o - LANES
    n_cols = o
    assert n_cols % TN_IN == 0

    def pack_w_in(l):
        return jnp.concatenate([w_in[l, :, s0:s1].astype(BF16) for s0, s1 in pieces], axis=1)

    x = jnp.concatenate([x_prompt.reshape(mp, d), x_sample.reshape(ms, d)], axis=0)
    row_cond = jnp.concatenate([jnp.zeros((mp,), jnp.int32),
                                1 + jnp.arange(ms, dtype=jnp.int32) // ts])
    rsel = (row_cond[:, None] == jnp.arange(MOD_ROWS, dtype=jnp.int32)[None, :]).astype(BF16)
    c_all = jnp.concatenate([c_prompt, c_sample,
                             jnp.zeros((MOD_ROWS - bp - bs, d), F32)], axis=0)
    mod = _modulation(c_all, w_ada, b_ada)

    zeros = lambda *s: jnp.zeros(s, F32)
    pad_m = lambda mm: jnp.pad(mm, ((0, 0), (0, LANES - mm.shape[1])))[:, None, :]
    outs_p = [[] for _ in range(6)]
    outs_s = [[] for _ in range(6)]

    w_br_a16, w_br_b16, w_br_c16 = w_br_a.astype(BF16), w_br_b.astype(BF16), w_br_c.astype(BF16)
    w_out16 = w_out.astype(BF16)

    for l in range(depth):
        mod_l = mod[l:l + 1]
        h = _norm_mod(x, rsel, norm_mix[l], mod_l, 0, 1)
        p = _matmul(h, pack_w_in(l), F32, TM_BIG, TN_IN)

        bias_if = jnp.pad(b_gate_a[l].reshape(1, 2 * nh_a), ((0, 0), (0, LANES - 2 * nh_a)))
        w_gk2p = jnp.pad(w_gk2[l], ((lr_lane, LANES - lr_lane - gate_rank), (0, 0)))
        groups = (
            dict(row0=0, B=bp, T=tp, La=64, Lg=64,
                 conv0=zeros(bp, CONV_W - 1, 2 * qk_a), c0=zeros(bp, nh_a, dv_a, dk_a),
                 n0=zeros(bp, nh_a, dk_a), m0=zeros(bp, 1, LANES),
                 sb0=zeros(bp, nh_b, dv_b, dk_b), sc0=zeros(bp, nh_c, dv_c, dk_c)),
            dict(row0=mp, B=bs, T=ts, La=ts, Lg=ts,
                 conv0=state_a_conv[l], c0=state_a_C[l], n0=state_a_n[l],
                 m0=pad_m(state_a_m[l]),
                 sb0=jnp.swapaxes(state_b_S[l], -1, -2), sc0=jnp.swapaxes(state_c_S[l], -1, -2)),
        )
        o_a = o_b = o_c = None
        for g, outs in zip(groups, (outs_p, outs_s)):
            o_a, a_c, a_n, a_m, a_conv = _mlstm(
                p, (col["a_qk"], col["a_v"], col["a_o"], col["a_if"]),
                g["conv0"], g["c0"], g["n0"], g["m0"], conv_w[l], conv_b[l].reshape(1, -1),
                bias_if, hn_a[l].reshape(1, -1), o_a,
                row0=g["row0"], B=g["B"], T=g["T"], L=g["La"], NH=nh_a, DK=dk_a, DV=dv_a)
            o_b, b_st = _gla_b(
                p, (col["b_q"], col["b_k"], col["b_v"], col["b_g"], col["b_lr"]),
                g["sb0"], w_gk2p, b_gk[l].reshape(1, -1), hn_b[l].reshape(1, -1), o_b,
                row0=g["row0"], B=g["B"], T=g["T"], L=g["Lg"], NH=nh_b, DK=dk_b, DV=dv_b)
            o_c, c_st = _gla_c(
                p, (col["c_q"], col["c_f"], col["c_i"], col["c_g"]),
                g["sc0"], lb_logits, hn_c[l].reshape(1, -1), o_c,
                layer=l, row0=g["row0"], B=g["B"], T=g["T"], L=g["Lg"],
                NH=nh_c, DK=dk_c, DV=dv_c, HP=4)
            for lst, val in zip(outs, (a_c, a_n, a_m[:, 0, :nh_a], a_conv,
                                       jnp.swapaxes(b_st, -1, -2), jnp.swapaxes(c_st, -1, -2))):
                lst.append(val)

        merged = _merge(o_a, o_b, o_c, w_br_a16, w_br_b16, w_br_c16, l, p, col["gates"])
        x = _out_proj(merged, w_out16, l, x, rsel, mod_l, 2)

        w_r = jnp.pad(jnp.concatenate([w_rg[l], w_re[l]], axis=1),
                      ((0, 0), (0, LANES - N_GROUPS - N_EXPERTS)))
        b_r = jnp.pad(jnp.concatenate([b_rg[l], b_re[l]]),
                      (0, LANES - N_GROUPS - N_EXPERTS)).reshape(1, LANES)
        h2, logits = _norm_mod_router(x, rsel, norm_moe[l], mod_l, 3, 4, w_r, b_r)
        eid, wts = _route(logits)
        rows, roww, blk_e, n_act, slot = _dispatch(eid, wts)
        yb = _expert_mlp(h2[rows], roww, blk_e, n_act, w_exp1, w_exp3, w_exp2, l)
        x = _gated_add(x, yb[slot[:, 0]], yb[slot[:, 1]], rsel, mod_l, 5)

    y_prompt = _final_norm(x, norm_final, 0, mp, 512).reshape(bp, tp, d)
    y_sample = _final_norm(x, norm_final, mp, ms, ms).reshape(bs, ts, d)
    stack = lambda lst: jnp.stack(lst)
    return (y_prompt, y_sample,
            stack(outs_p[0]), stack(outs_p[1]), stack(outs_p[2]), stack(outs_p[3]),
            stack(outs_p[4]), stack(outs_p[5]),
            stack(outs_s[0]), stack(outs_s[1]), stack(outs_s[2]), stack(outs_s[3]),
            stack(outs_s[4]), stack(outs_s[5]))
```

```python
import functools

import jax
import jax.numpy as jnp
import numpy as np
from jax import lax
from jax.experimental import pallas as pl
from jax.experimental.pallas import tpu as pltpu

F32 = jnp.float32
BF16 = jnp.bfloat16

NORM_EPS = 1e-6
NEG_BIG = -1e30
F_TINY = 1e-30
GLA_NORM = 16.0
CONV_W = 4
TOP_K = 2
N_GROUPS = 4
EXPERTS_PER_GROUP = 8
N_EXPERTS = N_GROUPS * EXPERTS_PER_GROUP
N_BRANCH = 3

LANES = 128
MOD_ROWS = 16
VMEM_LIMIT = 56 * 1024 * 1024

TM_BIG = 1040
TM_SMALL = 520
TN_IN = 768
MOE_ROWS = 256
SUB = 16
SAFE_LOG = 60.0


def _cparams(sem):
    return pltpu.CompilerParams(dimension_semantics=sem, vmem_limit_bytes=VMEM_LIMIT)


def _dot(a, b):
    return jnp.dot(a, b, preferred_element_type=F32)


def _dot_nt(a, b):
    return lax.dot_general(a, b, (((1,), (1,)), ((), ())), preferred_element_type=F32)


def _dot_tn(a, b):
    return lax.dot_general(a, b, (((0,), (0,)), ((), ())), preferred_element_type=F32)


def _split3(x):
    hi = x.astype(BF16)
    r = x - hi.astype(F32)
    mid = r.astype(BF16)
    lo = (r - mid.astype(F32)).astype(BF16)
    return hi, mid, lo


def _dot_sel(sel, x):
    hi, mid, lo = _split3(x)
    return _dot(sel, hi) + _dot(sel, mid) + _dot(sel, lo)


def _dot_nt_sel(sel, x):
    hi, mid, lo = _split3(x)
    return _dot_nt(sel, hi) + _dot_nt(sel, mid) + _dot_nt(sel, lo)


def _dot_nt_f32(a, b):
    a_hi = a.astype(BF16)
    a_lo = (a - a_hi.astype(F32)).astype(BF16)
    b_hi = b.astype(BF16)
    b_lo = (b - b_hi.astype(F32)).astype(BF16)
    return _dot_nt(a_hi, b_hi) + _dot_nt(a_hi, b_lo) + _dot_nt(a_lo, b_hi)


def _r16(x):
    return x.astype(BF16).astype(F32)


def _pack_bf16_pairs(xb):
    n = xb.shape[1] // 2
    bits = pltpu.bitcast(xb.astype(F32), jnp.int32)
    return lax.shift_right_logical(bits[:, :n], 16) | bits[:, n:]


def _unpack_bf16_pairs(w):
    lo = pltpu.bitcast(lax.shift_left(w, 16), F32).astype(BF16)
    hi = pltpu.bitcast(w & jnp.int32(-65536), F32).astype(BF16)
    return lo, hi


def _sigmoid(x):
    return jax.nn.sigmoid(x)


def _silu(x):
    return x * jax.nn.sigmoid(x)


def _log_sigmoid(x):
    return jnp.minimum(x, 0.0) - jnp.log1p(jnp.exp(-jnp.abs(x)))


def _rms(x):
    return x * lax.rsqrt(jnp.mean(x * x, axis=-1, keepdims=True) + NORM_EPS)


def _tril(n):
    r = lax.broadcasted_iota(jnp.int32, (n, n), 0)
    c = lax.broadcasted_iota(jnp.int32, (n, n), 1)
    return jnp.where(r >= c, 1.0, 0.0).astype(BF16)


def _eye(n):
    r = lax.broadcasted_iota(jnp.int32, (n, n), 0)
    c = lax.broadcasted_iota(jnp.int32, (n, n), 1)
    return jnp.where(r == c, 1.0, 0.0).astype(BF16)


def _mod_kernel(c_ref, w_ref, b_ref, o_ref):
    c = c_ref[...]
    o_ref[0] = _dot(_silu(c).astype(BF16), w_ref[0].astype(BF16)) + b_ref[0]


def _modulation(c_all, w_ada, b_ada):
    depth, d, n = w_ada.shape
    tn = 1024
    return pl.pallas_call(
        _mod_kernel,
        grid=(depth, n // tn),
        in_specs=[pl.BlockSpec((MOD_ROWS, d), lambda l, j: (0, 0)),
                  pl.BlockSpec((1, d, tn), lambda l, j: (l, 0, j)),
                  pl.BlockSpec((1, 1, tn), lambda l, j: (l, 0, j))],
        out_specs=pl.BlockSpec((1, MOD_ROWS, tn), lambda l, j: (l, 0, j)),
        out_shape=jax.ShapeDtypeStruct((depth, MOD_ROWS, n), F32),
        compiler_params=_cparams(("arbitrary", "arbitrary")),
        name="modulation",
    )(c_all, w_ada, b_ada.reshape(depth, 1, n))


def _row_mod(mixed, r_ref, m_ref):
    if mixed:
        return _dot_sel(r_ref[...], m_ref[0])
    return m_ref[0, 0:1, :]


def _norm_mod_kernel(x_ref, r_ref, g_ref, sh_ref, sc_ref, o_ref):
    last = pl.num_programs(0) - 1

    def body(mixed):
        y = _rms(x_ref[...]) * g_ref[...]
        o_ref[...] = (y * (1.0 + _row_mod(mixed, r_ref, sc_ref)) + _row_mod(mixed, r_ref, sh_ref)
                      ).astype(o_ref.dtype)

    pl.when(pl.program_id(0) != last)(lambda: body(False))
    pl.when(pl.program_id(0) == last)(lambda: body(True))


def _norm_mod(x, rsel, gain, mod_l, sh_idx, sc_idx):
    m, d = x.shape
    tm = TM_SMALL
    return pl.pallas_call(
        _norm_mod_kernel,
        grid=(m // tm,),
        in_specs=[pl.BlockSpec((tm, d), lambda i: (i, 0)),
                  pl.BlockSpec((tm, MOD_ROWS), lambda i: (i, 0)),
                  pl.BlockSpec((1, d), lambda i: (0, 0)),
                  pl.BlockSpec((1, MOD_ROWS, d), lambda i: (0, 0, sh_idx)),
                  pl.BlockSpec((1, MOD_ROWS, d), lambda i: (0, 0, sc_idx))],
        out_specs=pl.BlockSpec((tm, d), lambda i: (i, 0)),
        out_shape=jax.ShapeDtypeStruct((m, d), BF16),
        compiler_params=_cparams(("arbitrary",)),
        name="norm_mod",
    )(x, rsel, gain.reshape(1, d), mod_l, mod_l)


def _norm_mod_router_kernel(x_ref, r_ref, g_ref, sh_ref, sc_ref, wr_ref, br_ref, o_ref, lg_ref):
    last = pl.num_programs(0) - 1

    def body(mixed):
        y = _rms(x_ref[...]) * g_ref[...]
        h = y * (1.0 + _row_mod(mixed, r_ref, sc_ref)) + _row_mod(mixed, r_ref, sh_ref)
        hb = h.astype(BF16)
        o_ref[...] = _pack_bf16_pairs(hb)
        lg_ref[...] = _dot(hb, wr_ref[...].astype(BF16)) + br_ref[...]

    pl.when(pl.program_id(0) != last)(lambda: body(False))
    pl.when(pl.program_id(0) == last)(lambda: body(True))


def _norm_mod_router(x, rsel, gain, mod_l, sh_idx, sc_idx, w_r, b_r):
    m, d = x.shape
    tm = TM_SMALL
    return pl.pallas_call(
        _norm_mod_router_kernel,
        grid=(m // tm,),
        in_specs=[pl.BlockSpec((tm, d), lambda i: (i, 0)),
                  pl.BlockSpec((tm, MOD_ROWS), lambda i: (i, 0)),
                  pl.BlockSpec((1, d), lambda i: (0, 0)),
                  pl.BlockSpec((1, MOD_ROWS, d), lambda i: (0, 0, sh_idx)),
                  pl.BlockSpec((1, MOD_ROWS, d), lambda i: (0, 0, sc_idx)),
                  pl.BlockSpec((d, LANES), lambda i: (0, 0)),
                  pl.BlockSpec((1, LANES), lambda i: (0, 0))],
        out_specs=[pl.BlockSpec((tm, d // 2), lambda i: (i, 0)),
                   pl.BlockSpec((tm, LANES), lambda i: (i, 0))],
        out_shape=[jax.ShapeDtypeStruct((m, d // 2), jnp.int32),
                   jax.ShapeDtypeStruct((m, LANES), F32)],
        compiler_params=_cparams(("arbitrary",)),
        name="norm_mod_router",
    )(x, rsel, gain.reshape(1, d), mod_l, mod_l, w_r, b_r)


def _final_norm_kernel(x_ref, g_ref, o_ref):
    o_ref[...] = _rms(x_ref[...]) * g_ref[...]


def _final_norm(x, gain, row0, rows, tm):
    d = x.shape[1]
    blk0 = row0 // tm
    return pl.pallas_call(
        _final_norm_kernel,
        grid=(rows // tm,),
        in_specs=[pl.BlockSpec((tm, d), lambda i: (blk0 + i, 0)),
                  pl.BlockSpec((1, d), lambda i: (0, 0))],
        out_specs=pl.BlockSpec((tm, d), lambda i: (i, 0)),
        out_shape=jax.ShapeDtypeStruct((rows, d), F32),
        compiler_params=_cparams(("arbitrary",)),
        name="final_norm",
    )(x, gain.reshape(1, d))


def _matmul_kernel(a_ref, b_ref, o_ref):
    o_ref[...] = _dot(a_ref[...], b_ref[...]).astype(o_ref.dtype)


def _matmul(a, b, out_dtype, tm, tn):
    m, k = a.shape
    n = b.shape[1]
    return pl.pallas_call(
        _matmul_kernel,
        grid=(m // tm, n // tn),
        in_specs=[pl.BlockSpec((tm, k), lambda i, j: (i, 0)),
                  pl.BlockSpec((k, tn), lambda i, j: (0, j))],
        out_specs=pl.BlockSpec((tm, tn), lambda i, j: (i, j)),
        out_shape=jax.ShapeDtypeStruct((m, n), out_dtype),
        compiler_params=_cparams(("arbitrary", "arbitrary")),
        name="in_proj",
    )(a, b)


def _merge_kernel(oa_ref, ob_ref, oc_ref, wa_ref, wb_ref, wc_ref, ga_ref, gb_ref, gc_ref, o_ref):
    acc = _sigmoid(ga_ref[...]) * _dot(oa_ref[...], wa_ref[...])
    acc = acc + _sigmoid(gb_ref[...]) * _dot(ob_ref[...], wb_ref[...])
    acc = acc + _sigmoid(gc_ref[...]) * _dot(oc_ref[...], wc_ref[...])
    o_ref[...] = acc.astype(o_ref.dtype)


def _merge(o_a, o_b, o_c, w_a, w_b, w_c, layer, p, gate_col0):
    m, kw = o_a.shape
    d = w_a.shape[2]
    tm, tn = TM_SMALL, 512
    g0 = gate_col0 // tn
    gstride = d // tn
    o_spec = pl.BlockSpec((tm, kw), lambda j, i: (i, 0))
    w_spec = pl.BlockSpec((None, kw, tn), lambda j, i: (layer, 0, j))

    def g_spec(br):
        return pl.BlockSpec((tm, tn), lambda j, i: (i, g0 + br * gstride + j))

    return pl.pallas_call(
        _merge_kernel,
        grid=(d // tn, m // tm),
        in_specs=[o_spec, o_spec, o_spec, w_spec, w_spec, w_spec, g_spec(0), g_spec(1), g_spec(2)],
        out_specs=pl.BlockSpec((tm, tn), lambda j, i: (i, j)),
        out_shape=jax.ShapeDtypeStruct((m, d), BF16),
        compiler_params=_cparams(("arbitrary", "arbitrary")),
        name="merge",
    )(o_a, o_b, o_c, w_a, w_b, w_c, p, p, p)


def _out_proj_kernel(a_ref, w_ref, x_ref, r_ref, g_ref, o_ref):
    last = pl.num_programs(0) - 1
    acc = _dot(a_ref[...], w_ref[...])

    def body(mixed):
        o_ref[...] = x_ref[...] + _row_mod(mixed, r_ref, g_ref) * acc

    pl.when(pl.program_id(0) != last)(lambda: body(False))
    pl.when(pl.program_id(0) == last)(lambda: body(True))


def _out_proj(a, w, layer, x, rsel, mod_l, g_idx):
    m, k = a.shape
    d = w.shape[2]
    tm, tn = TM_BIG, 512
    nj = d // tn
    return pl.pallas_call(
        _out_proj_kernel,
        grid=(m // tm, nj),
        in_specs=[pl.BlockSpec((tm, k), lambda i, j: (i, 0)),
                  pl.BlockSpec((None, k, tn), lambda i, j: (layer, 0, j)),
                  pl.BlockSpec((tm, tn), lambda i, j: (i, j)),
                  pl.BlockSpec((tm, MOD_ROWS), lambda i, j: (i, 0)),
                  pl.BlockSpec((1, MOD_ROWS, tn), lambda i, j: (0, 0, g_idx * nj + j))],
        out_specs=pl.BlockSpec((tm, tn), lambda i, j: (i, j)),
        out_shape=jax.ShapeDtypeStruct((m, d), F32),
        compiler_params=_cparams(("arbitrary", "arbitrary")),
        name="out_proj",
    )(a, w, x, rsel, mod_l)


def _gated_add_kernel(x_ref, y0_ref, y1_ref, r_ref, g_ref, o_ref):
    last = pl.num_programs(0) - 1

    def body(mixed):
        o_ref[...] = x_ref[...] + _row_mod(mixed, r_ref, g_ref) * (y0_ref[...] + y1_ref[...])

    pl.when(pl.program_id(0) != last)(lambda: body(False))
    pl.when(pl.program_id(0) == last)(lambda: body(True))


def _gated_add(x, y0, y1, rsel, mod_l, g_idx):
    m, d = x.shape
    tm, tn = TM_SMALL, 1024
    nj = d // tn
    spec = pl.BlockSpec((tm, tn), lambda i, j: (i, j))
    return pl.pallas_call(
        _gated_add_kernel,
        grid=(m // tm, nj),
        in_specs=[spec, spec, spec,
                  pl.BlockSpec((tm, MOD_ROWS), lambda i, j: (i, 0)),
                  pl.BlockSpec((1, MOD_ROWS, tn), lambda i, j: (0, 0, g_idx * nj + j))],
        out_specs=spec,
        out_shape=jax.ShapeDtypeStruct((m, d), F32),
        compiler_params=_cparams(("arbitrary", "arbitrary")),
        name="moe_combine",
    )(x, y0, y1, rsel, mod_l)


def _alias_prev(kern, in_specs, args, o_prev):
    if o_prev is None:
        return kern, in_specs, args, {}
    n = len(args)

    def wrapped(*refs):
        return kern(*refs[:n], *refs[n + 1:])

    return wrapped, in_specs + [pl.BlockSpec(memory_space=pl.ANY)], args + [o_prev], {n: 0}


def _mlstm_kernel(qp_ref, kp_ref, v_ref, og_ref, if_ref, conv0_ref, c0_ref, n0_ref, m0_ref,
                  cw_ref, cb_ref, bif_ref, hn_ref,
                  o_ref, cout_ref, nout_ref, mout_ref, convout_ref,
                  c_s, n_s, m_s, ubuf, q_s, k_s, *, L, NH, DK, DV):
    ci = pl.program_id(1)
    QK = NH * DK

    @pl.when(ci == 0)
    def _():
        c_s[...] = c0_ref[0]
        n_s[...] = n0_ref[0]
        m_s[...] = m0_ref[0]
        ubuf[8 - (CONV_W - 1):8, :] = conv0_ref[0]

    ubuf[8:8 + L, 0:QK] = qp_ref[...]
    ubuf[8:8 + L, QK:2 * QK] = kp_ref[...]
    y = cb_ref[...]
    for j in range(CONV_W):
        y = y + ubuf[8 - (CONV_W - 1) + j:8 - (CONV_W - 1) + j + L, :] * cw_ref[j:j + 1, :]
    ubuf[0:8, :] = ubuf[L:L + 8, :]
    qk = _silu(y)
    q_s[...] = qk[:, 0:QK] * (DK ** -0.5)
    k_s[...] = qk[:, QK:2 * QK]

    ifv = if_ref[...] + bif_ref[...]
    b_all = _dot_sel(_tril(L), _log_sigmoid(ifv))
    eye = _eye(LANES)
    ig_t = _dot_nt_sel(eye, ifv)
    b_t = _dot_nt_sel(eye, b_all)
    row = lax.broadcasted_iota(jnp.int32, (L, L), 0)
    col = lax.broadcasted_iota(jnp.int32, (L, L), 1)
    causal = row >= col
    lane = lax.broadcasted_iota(jnp.int32, (1, LANES), 1)
    m_old = m_s[...]
    m_new = m_old

    for h in range(NH):
        b_c = b_all[:, NH + h:NH + h + 1]
        ig_c = ifv[:, h:h + 1]
        b_r = b_t[NH + h:NH + h + 1, :]
        ig_r = ig_t[h:h + 1, :]
        m_prev = m_old[:, h:h + 1]
        log_d = jnp.where(causal, b_c - b_r + ig_r, NEG_BIG)
        inter = b_c + m_prev
        mt = jnp.maximum(jnp.max(log_d, axis=-1, keepdims=True), inter)
        dm = jnp.exp(log_d - mt)
        sc = jnp.exp(inter - mt)
        qh = q_s[:, h * DK:(h + 1) * DK]
        kh = k_s[:, h * DK:(h + 1) * DK]
        vh = v_ref[:, h * DV:(h + 1) * DV]
        qb, kb, vb = qh.astype(BF16), kh.astype(BF16), vh.astype(BF16)
        c_h = c_s[h]
        n_h = n_s[h:h + 1, :]
        s = _dot_nt(qb, kb) * dm
        num = sc * _dot_nt(qb, c_h.astype(BF16)) + _dot(s.astype(BF16), vb)
        den = (sc * jnp.sum(qb.astype(F32) * _r16(n_h), axis=-1, keepdims=True)
               + jnp.sum(s, axis=-1, keepdims=True))
        hc = num / jnp.maximum(jnp.abs(den), jnp.exp(-mt))
        mt_l = mt[L - 1:L, :]
        w_l = jnp.exp((b_c[L - 1:L, :] - b_c) + ig_c - mt_l)
        s_l = sc[L - 1:L, :]
        c_s[h] = s_l * c_h + _dot_tn(vb, (kh * w_l).astype(BF16))
        n_s[h:h + 1, :] = s_l * n_h + jnp.sum(_r16(w_l) * kb.astype(F32), axis=0, keepdims=True)
        m_new = jnp.where(lane == h, mt_l, m_new)
        out = _rms(hc) * hn_ref[:, h * DV:(h + 1) * DV] * _sigmoid(og_ref[:, h * DV:(h + 1) * DV])
        o_ref[:, h * DV:(h + 1) * DV] = out.astype(o_ref.dtype)

    m_s[...] = m_new

    @pl.when(ci == pl.num_programs(1) - 1)
    def _():
        cout_ref[0] = c_s[...]
        nout_ref[0] = n_s[...]
        mout_ref[0] = m_s[...]
        convout_ref[0] = ubuf[8 - (CONV_W - 1):8, :]


def _mlstm(p, cols, conv0, c0, n0, m0, conv_w, conv_b, bias_if, hn, o_prev,
           *, row0, B, T, L, NH, DK, DV):
    QK, W = NH * DK, NH * DV
    nC = T // L
    rb0 = row0 // L
    c_qk, c_v, c_o, c_if = cols

    def rows(b, c):
        return rb0 + b * nC + c

    kern = functools.partial(_mlstm_kernel, L=L, NH=NH, DK=DK, DV=DV)
    in_specs = [pl.BlockSpec((L, QK), lambda b, c: (rows(b, c), c_qk // QK)),
                  pl.BlockSpec((L, QK), lambda b, c: (rows(b, c), c_qk // QK + 1)),
                  pl.BlockSpec((L, W), lambda b, c: (rows(b, c), c_v // W)),
                  pl.BlockSpec((L, W), lambda b, c: (rows(b, c), c_o // W)),
                  pl.BlockSpec((L, LANES), lambda b, c: (rows(b, c), c_if // LANES)),
                  pl.BlockSpec((1, CONV_W - 1, 2 * QK), lambda b, c: (b, 0, 0)),
                  pl.BlockSpec((1, NH, DV, DK), lambda b, c: (b, 0, 0, 0)),
                  pl.BlockSpec((1, NH, DK), lambda b, c: (b, 0, 0)),
                  pl.BlockSpec((1, 1, LANES), lambda b, c: (b, 0, 0)),
                  pl.BlockSpec((CONV_W, 2 * QK), lambda b, c: (0, 0)),
                  pl.BlockSpec((1, 2 * QK), lambda b, c: (0, 0)),
                  pl.BlockSpec((1, LANES), lambda b, c: (0, 0)),
                  pl.BlockSpec((1, W), lambda b, c: (0, 0))]
    args = [p, p, p, p, p, conv0, c0, n0, m0, conv_w, conv_b, bias_if, hn]
    kern, in_specs, args, aliases = _alias_prev(kern, in_specs, args, o_prev)
    return pl.pallas_call(
        kern,
        grid=(B, nC),
        in_specs=in_specs,
        out_specs=[pl.BlockSpec((L, W), lambda b, c: (rows(b, c), 0)),
                   pl.BlockSpec((1, NH, DV, DK), lambda b, c: (b, 0, 0, 0)),
                   pl.BlockSpec((1, NH, DK), lambda b, c: (b, 0, 0)),
                   pl.BlockSpec((1, 1, LANES), lambda b, c: (b, 0, 0)),
                   pl.BlockSpec((1, CONV_W - 1, 2 * QK), lambda b, c: (b, 0, 0))],
        out_shape=[jax.ShapeDtypeStruct((p.shape[0], W), BF16),
                   jax.ShapeDtypeStruct((B, NH, DV, DK), F32),
                   jax.ShapeDtypeStruct((B, NH, DK), F32),
                   jax.ShapeDtypeStruct((B, 1, LANES), F32),
                   jax.ShapeDtypeStruct((B, CONV_W - 1, 2 * QK), F32)],
        scratch_shapes=[pltpu.VMEM((NH, DV, DK), F32),
                        pltpu.VMEM((NH, DK), F32),
                        pltpu.VMEM((1, LANES), F32),
                        pltpu.VMEM((L + 8, 2 * QK), F32),
                        pltpu.VMEM((L, QK), F32),
                        pltpu.VMEM((L, QK), F32)],
        input_output_aliases=aliases,
        compiler_params=_cparams(("arbitrary", "arbitrary")),
        name="mixer_mlstm",
    )(*args)


def _gla_intra_exact(q, k, b, L, c):
    dk = q.shape[1]
    nsub = L // c
    q3 = q.reshape(nsub, c, dk)
    k3 = k.reshape(nsub, c, dk)
    b3 = b.reshape(nsub, c, dk)
    t_idx = lax.broadcasted_iota(jnp.int32, (1, c, 1), 1)
    s_idx = lax.broadcasted_iota(jnp.int32, (1, 1, c), 2)
    a_diag = jnp.zeros((nsub, c, c), F32)
    for s in range(c):
        arg = jnp.where(t_idx >= s, b3 - b3[:, s:s + 1, :], NEG_BIG)
        col_s = jnp.sum(q3 * k3[:, s:s + 1, :] * jnp.exp(arg), axis=-1, keepdims=True)
        a_diag = jnp.where(s_idx == s, col_s, a_diag)
    a_diag = a_diag.reshape(L, c)
    if nsub == 1:
        return a_diag
    row = lax.broadcasted_iota(jnp.int32, (L, L), 0)
    col = lax.broadcasted_iota(jnp.int32, (L, L), 1)
    rep_r = lax.broadcasted_iota(jnp.int32, (c, L), 0)
    rep_c = lax.broadcasted_iota(jnp.int32, (c, L), 1)
    rep = jnp.where((rep_c & (c - 1)) == rep_r, 1.0, 0.0).astype(BF16)
    a = jnp.where((row & -c) == (col & -c), _dot(a_diag.astype(BF16), rep), 0.0)
    blocks = [jnp.zeros((c, L), F32)]
    for i in range(1, nsub):
        r_i = b[i * c - 1:i * c, :]
        q_i = q[i * c:(i + 1) * c, :] * jnp.exp(b[i * c:(i + 1) * c, :] - r_i)
        k_i = k * jnp.exp(jnp.minimum(r_i - b, 0.0))
        blocks.append(_dot_nt_f32(q_i, k_i))
    return a + jnp.where((col & -c) < (row & -c), jnp.concatenate(blocks, axis=0), 0.0)


def _gla_intra_fact(q, k, b, L, c):
    nsub = L // c
    row = lax.broadcasted_iota(jnp.int32, (L, L), 0)
    col = lax.broadcasted_iota(jnp.int32, (L, L), 1)
    blocks = []
    for i in range(nsub):
        b_i = b[i * c:(i + 1) * c, :]
        if i == 0:
            q_i = q[0:c, :] * jnp.exp(b_i)
            k_i = k * jnp.exp(jnp.minimum(-b, SAFE_LOG))
        else:
            r_i = b[i * c - 1:i * c, :]
            q_i = q[i * c:(i + 1) * c, :] * jnp.exp(b_i - r_i)
            k_i = k * jnp.exp(jnp.minimum(r_i - b, SAFE_LOG))
        blocks.append(_dot_nt_f32(q_i, k_i))
    a = blocks[0] if nsub == 1 else jnp.concatenate(blocks, axis=0)
    return jnp.where(col <= row, a, 0.0)


def _gla_cum_decay(g, L):
    c = min(SUB, L)
    nsub = L // c
    b = _dot_sel(_tril(L), g)
    drops = [b[c - 1:c, :]] + [b[(i + 1) * c - 1:(i + 1) * c, :] - b[i * c - 1:i * c, :]
                               for i in range(1, nsub)]
    return b, jnp.min(drops[0] if nsub == 1 else jnp.concatenate(drops, axis=0))


def _either(pred, body):
    pl.when(pred)(lambda: body(True))
    pl.when(jnp.logical_not(pred))(lambda: body(False))


def _gla_chunk(q, k, v, b, st, L, fact):
    c = min(SUB, L)
    b_l = b[L - 1:L, :]
    o = _dot_nt((q * jnp.exp(b)).astype(BF16), st.astype(BF16))
    a = _gla_intra_fact(q, k, b, L, c) if fact else _gla_intra_exact(q, k, b, L, c)
    vb = v.astype(BF16)
    o = o + _dot(a.astype(BF16), vb)
    st_new = st * jnp.exp(b_l) + _dot_tn(vb, (k * jnp.exp(b_l - b)).astype(BF16))
    return o, st_new


def _gla_b_kernel(q_ref, k_ref, v_ref, gt_ref, lr_ref, s0_ref, w2_ref, bgk_ref, hn_ref,
                  o_ref, sout_ref, st_s, *, L, HP, DK, DV):
    ci = pl.program_id(2)

    @pl.when(ci == 0)
    def _():
        st_s[...] = s0_ref[0]

    z = _dot(lr_ref[...].astype(BF16), w2_ref[...].astype(BF16)) + bgk_ref[...]
    b, worst = _gla_cum_decay(_log_sigmoid(z) / GLA_NORM, L)

    def body(fact):
        for h in range(HP):
            sl = slice(h * DK, (h + 1) * DK)
            sv = slice(h * DV, (h + 1) * DV)
            o, st_new = _gla_chunk(q_ref[:, sl] * (DK ** -0.5), k_ref[:, sl], v_ref[:, sv],
                                   b[:, sl], st_s[h], L, fact)
            st_s[h] = st_new
            o_ref[:, sv] = (_rms(o) * hn_ref[:, sv] * _silu(gt_ref[:, sv])).astype(o_ref.dtype)

    _either(worst > -SAFE_LOG, body)

    @pl.when(ci == pl.num_programs(2) - 1)
    def _():
        sout_ref[0] = st_s[...]


def _gla_b(p, cols, s0t, w_gk2p, b_gk, hn, o_prev, *, row0, B, T, L, NH, DK, DV, HP):
    nC = T // L
    rb0 = row0 // L
    c_q, c_k, c_v, c_g, c_lr = cols
    wq, wv = HP * DK, HP * DV

    def rows(b, c):
        return rb0 + b * nC + c

    kern = functools.partial(_gla_b_kernel, L=L, HP=HP, DK=DK, DV=DV)
    in_specs = [pl.BlockSpec((L, wq), lambda b, g, c: (rows(b, c), c_q // wq + g)),
                pl.BlockSpec((L, wq), lambda b, g, c: (rows(b, c), c_k // wq + g)),
                pl.BlockSpec((L, wv), lambda b, g, c: (rows(b, c), c_v // wv + g)),
                pl.BlockSpec((L, wv), lambda b, g, c: (rows(b, c), c_g // wv + g)),
                pl.BlockSpec((L, LANES), lambda b, g, c: (rows(b, c), c_lr // LANES)),
                pl.BlockSpec((1, HP, DV, DK), lambda b, g, c: (b, g, 0, 0)),
                pl.BlockSpec((LANES, wq), lambda b, g, c: (0, g)),
                pl.BlockSpec((1, wq), lambda b, g, c: (0, g)),
                pl.BlockSpec((1, wv), lambda b, g, c: (0, g))]
    args = [p, p, p, p, p, s0t, w_gk2p, b_gk, hn]
    kern, in_specs, args, aliases = _alias_prev(kern, in_specs, args, o_prev)
    return pl.pallas_call(
        kern,
        grid=(B, NH // HP, nC),
        in_specs=in_specs,
        out_specs=[pl.BlockSpec((L, wv), lambda b, g, c: (rows(b, c), g)),
                   pl.BlockSpec((1, HP, DV, DK), lambda b, g, c: (b, g, 0, 0))],
        out_shape=[jax.ShapeDtypeStruct((p.shape[0], NH * DV), BF16),
                   jax.ShapeDtypeStruct((B, NH, DV, DK), F32)],
        scratch_shapes=[pltpu.VMEM((HP, DV, DK), F32)],
        input_output_aliases=aliases,
        compiler_params=_cparams(("arbitrary", "arbitrary", "arbitrary")),
        name="mixer_gla",
    )(*args)


def _gla_c_kernel(q_ref, f_ref, i_ref, gt_ref, s0_ref, lbl_ref, hn_ref,
                  o_ref, sout_ref, st_s, *, L, HP, DK, DV, layer):
    ci = pl.program_id(2)

    @pl.when(ci == 0)
    def _():
        st_s[...] = s0_ref[0]

    lbl = lbl_ref[...]
    e = jnp.exp(lbl - jnp.max(lbl, axis=0, keepdims=True))
    sm = e / jnp.sum(e, axis=0, keepdims=True)
    lb = jnp.sum(sm[0:layer + 1, :], axis=0, keepdims=True) - sm[0:1, :]

    f = lb + (1.0 - lb) * _sigmoid(f_ref[...])
    b, worst = _gla_cum_decay(jnp.log(jnp.maximum(f, F_TINY)), L)

    def body(fact):
        for h in range(HP):
            sl = slice(h * DK, (h + 1) * DK)
            sv = slice(h * DV, (h + 1) * DV)
            o, st_new = _gla_chunk(_silu(q_ref[:, sl]), 1.0 - f[:, sl], i_ref[:, sv], b[:, sl],
                                   st_s[h], L, fact)
            st_s[h] = st_new
            o_ref[:, sv] = (_rms(o) * hn_ref[:, sv] * _silu(gt_ref[:, sv])).astype(o_ref.dtype)

    _either(worst > -SAFE_LOG, body)

    @pl.when(ci == pl.num_programs(2) - 1)
    def _():
        sout_ref[0] = st_s[...]


def _gla_c(p, cols, s0t, lb_logits, hn, o_prev, *, layer, row0, B, T, L, NH, DK, DV, HP):
    nC = T // L
    rb0 = row0 // L
    c_q, c_f, c_i, c_g = cols
    wq, wv = HP * DK, HP * DV

    def rows(b, c):
        return rb0 + b * nC + c

    kern = functools.partial(_gla_c_kernel, L=L, HP=HP, DK=DK, DV=DV, layer=layer)
    in_specs = [pl.BlockSpec((L, wq), lambda b, g, c: (rows(b, c), c_q // wq + g)),
                pl.BlockSpec((L, wq), lambda b, g, c: (rows(b, c), c_f // wq + g)),
                pl.BlockSpec((L, wv), lambda b, g, c: (rows(b, c), c_i // wv + g)),
                pl.BlockSpec((L, wv), lambda b, g, c: (rows(b, c), c_g // wv + g)),
                pl.BlockSpec((1, HP, DV, DK), lambda b, g, c: (b, g, 0, 0)),
                pl.BlockSpec((lb_logits.shape[0], wq), lambda b, g, c: (0, g)),
                pl.BlockSpec((1, wv), lambda b, g, c: (0, g))]
    args = [p, p, p, p, s0t, lb_logits, hn]
    kern, in_specs, args, aliases = _alias_prev(kern, in_specs, args, o_prev)
    return pl.pallas_call(
        kern,
        grid=(B, NH // HP, nC),
        in_specs=in_specs,
        out_specs=[pl.BlockSpec((L, wv), lambda b, g, c: (rows(b, c), g)),
                   pl.BlockSpec((1, HP, DV, DK), lambda b, g, c: (b, g, 0, 0))],
        out_shape=[jax.ShapeDtypeStruct((p.shape[0], NH * DV), BF16),
                   jax.ShapeDtypeStruct((B, NH, DV, DK), F32)],
        scratch_shapes=[pltpu.VMEM((HP, DV, DK), F32)],
        input_output_aliases=aliases,
        compiler_params=_cparams(("arbitrary", "arbitrary", "arbitrary")),
        name="mixer_hgrn2",
    )(*args)


def _new_expert(be_ref):
    b = pl.program_id(1)
    return (b == 0) | (be_ref[b] != be_ref[jnp.maximum(b - 1, 0)])


def _gmm1_kernel(be_ref, na_ref, x_ref, w1_ref, w3_ref, o_ref, w1_s, w3_s):
    @pl.when(_new_expert(be_ref))
    def _():
        w1_s[...] = w1_ref[...].astype(BF16)
        w3_s[...] = w3_ref[...].astype(BF16)

    @pl.when(pl.program_id(1) < na_ref[0])
    def _():
        x_lo, x_hi = _unpack_bf16_pairs(x_ref[...])
        half = x_lo.shape[1]
        up = _dot(x_lo, w1_s[0:half, :]) + _dot(x_hi, w1_s[half:, :])
        gate = _dot(x_lo, w3_s[0:half, :]) + _dot(x_hi, w3_s[half:, :])
        o_ref[...] = (_silu(up) * gate).astype(o_ref.dtype)

    @pl.when(pl.program_id(1) >= na_ref[0])
    def _():
        o_ref[...] = jnp.zeros_like(o_ref)


def _gmm2_kernel(be_ref, na_ref, h_ref, w2_ref, rw_ref, o_ref, w2_s):
    @pl.when(_new_expert(be_ref))
    def _():
        w2_s[...] = w2_ref[...].astype(BF16)

    @pl.when(pl.program_id(1) < na_ref[0])
    def _():
        o_ref[...] = _dot(h_ref[...], w2_s[...]) * rw_ref[...]

    @pl.when(pl.program_id(1) >= na_ref[0])
    def _():
        o_ref[...] = jnp.zeros_like(o_ref)


def _expert_mlp(xg, roww, blk_e, n_act, w1, w3, w2, layer):
    rows, dh = xg.shape
    d, f = w1.shape[2], w1.shape[3]
    r = MOE_ROWS
    nb = rows // r
    tf, td = f // 2, d // 2

    def act(b, na):
        return jnp.minimum(b, na[0] - 1)

    hb = pl.pallas_call(
        _gmm1_kernel,
        grid_spec=pltpu.PrefetchScalarGridSpec(
            num_scalar_prefetch=2, grid=(f // tf, nb),
            in_specs=[pl.BlockSpec((r, dh), lambda j, b, be, na: (act(b, na), 0)),
                      pl.BlockSpec((None, None, d, tf), lambda j, b, be, na: (layer, be[b], 0, j)),
                      pl.BlockSpec((None, None, d, tf), lambda j, b, be, na: (layer, be[b], 0, j))],
            out_specs=pl.BlockSpec((r, tf), lambda j, b, be, na: (b, j)),
            scratch_shapes=[pltpu.VMEM((d, tf), BF16), pltpu.VMEM((d, tf), BF16)]),
        out_shape=jax.ShapeDtypeStruct((rows, f), BF16),
        compiler_params=_cparams(("arbitrary", "arbitrary")),
        name="expert_up",
    )(blk_e, n_act, xg, w1, w3)
    return pl.pallas_call(
        _gmm2_kernel,
        grid_spec=pltpu.PrefetchScalarGridSpec(
            num_scalar_prefetch=2, grid=(d // td, nb),
            in_specs=[pl.BlockSpec((r, f), lambda j, b, be, na: (act(b, na), 0)),
                      pl.BlockSpec((None, None, f, td), lambda j, b, be, na: (layer, be[b], 0, j)),
                      pl.BlockSpec((r, 1), lambda j, b, be, na: (act(b, na), 0))],
            out_specs=pl.BlockSpec((r, td), lambda j, b, be, na: (b, j)),
            scratch_shapes=[pltpu.VMEM((f, td), BF16)]),
        out_shape=jax.ShapeDtypeStruct((rows, d), F32),
        compiler_params=_cparams(("arbitrary", "arbitrary")),
        name="expert_down",
    )(blk_e, n_act, hb, w2, roww)


def _route(logits):
    n = logits.shape[0]
    pg = jax.nn.softmax(logits[:, :N_GROUPS], axis=-1)
    grp = jnp.argmax(pg, axis=-1)
    p_grp = jnp.max(pg, axis=-1)
    le = logits[:, N_GROUPS:N_GROUPS + N_EXPERTS].reshape(n, N_GROUPS, EXPERTS_PER_GROUP)
    le_g = le[jnp.arange(n), grp]
    top_logit, top_j = lax.top_k(le_g, TOP_K)
    wts = jax.nn.softmax(top_logit, axis=-1) * p_grp[:, None]
    eid = (grp[:, None] * EXPERTS_PER_GROUP + top_j).astype(jnp.int32)
    return eid, wts


def _dispatch(eid, wts):
    n = eid.shape[0]
    a = n * TOP_K
    r = MOE_ROWS
    nb = -(-a // r) + N_EXPERTS
    flat_e = eid.reshape(a)
    order = jnp.argsort(flat_e)
    se = flat_e[order]
    counts = jnp.bincount(flat_e, length=N_EXPERTS)
    padded = (counts + r - 1) // r * r
    pend = jnp.cumsum(padded)
    pstart = pend - padded
    start = jnp.cumsum(counts) - counts
    dest = (pstart[se] + jnp.arange(a, dtype=jnp.int32) - start[se]).astype(jnp.int32)
    rows = jnp.zeros((nb * r,), jnp.int32).at[dest].set((order // TOP_K).astype(jnp.int32))
    roww = jnp.zeros((nb * r,), F32).at[dest].set(wts.reshape(a)[order])
    n_act = (pend[-1] // r).astype(jnp.int32)
    blk = jnp.arange(nb, dtype=jnp.int32)
    blk_e = jnp.minimum(jnp.searchsorted(pend, jnp.minimum(blk, n_act - 1) * r, side='right'),
                        N_EXPERTS - 1).astype(jnp.int32)
    slot = jnp.zeros((a,), jnp.int32).at[order].set(dest).reshape(n, TOP_K)
    return rows, roww.reshape(nb * r, 1), blk_e, n_act.reshape(1), slot


def kernel(x_prompt, x_sample, c_prompt, c_sample, state_a_C, state_a_n, state_a_m, state_a_conv, state_b_S, state_c_S, w_ada, b_ada, norm_mix, w_in, conv_w, conv_b, b_gate_a, hn_a, w_gk2, b_gk, hn_b, lb_logits, hn_c, w_br_a, w_br_b, w_br_c, w_out, norm_moe, w_rg, b_rg, w_re, b_re, w_exp1, w_exp3, w_exp2, norm_final):
    depth = w_ada.shape[0]
    bp, tp, d = x_prompt.shape
    bs, ts, _ = x_sample.shape
    nh_a, dv_a, dk_a = state_a_C.shape[2:]
    nh_b, dk_b, dv_b = state_b_S.shape[2:]
    nh_c, dk_c, dv_c = state_c_S.shape[2:]
    gate_rank = w_gk2.shape[1]
    qk_a, w_a = nh_a * dk_a, nh_a * dv_a
    qk_b, w_b = nh_b * dk_b, nh_b * dv_b
    qk_c, w_c = nh_c * dk_c, nh_c * dv_c
    mp, ms = bp * tp, bs * ts
    m = mp + ms
    assert bp == 1 and 1 + bs <= MOD_ROWS and m % TM_BIG == 0 and ms <= TM_SMALL

    sizes = (2 * qk_a, w_a, 2 * nh_a, w_a, qk_b, qk_b, w_b, gate_rank, w_b,
             qk_c, qk_c, w_c, w_c, N_BRANCH * d)
    names = ("a_qk", "a_v", "a_if", "a_o", "b_q", "b_k", "b_v", "b_lr", "b_g",
             "c_q", "c_f", "c_i", "c_g", "gates")
    src, o = {}, 0
    for nm, sz in zip(names, sizes):
        src[nm] = o
        o += sz
    n_src = o
    lr_lane = src["b_lr"] % LANES
    pieces = ((src["a_qk"], src["a_if"]), (src["a_o"], src["b_lr"]), (src["b_g"], n_src),
              (src["a_if"], src["a_if"] + LANES),
              (src["b_lr"] - lr_lane, src["b_lr"] - lr_lane + LANES))
    assert src["a_if"] % LANES == 0 and lr_lane + gate_rank <= LANES
    col, o = {}, 0
    for (s0, s1), members in zip(pieces, (("a_qk", "a_v"), ("a_o", "b_q", "b_k", "b_v"),
                                          ("b_g", "c_q", "c_f", "c_i", "c_g", "gates"),
                                          ("a_if",), ())):
        assert (s1 - s0) % LANES == 0
        for nm in members:
            col[nm] = o + src[nm] - s0
        o += s1 - s0
    col["b_lr"] = o - LANES
    n_cols = o
    assert n_cols % TN_IN == 0

    def pack_w_in(l):
        return jnp.concatenate([w_in[l, :, s0:s1].astype(BF16) for s0, s1 in pieces], axis=1)

    x = jnp.concatenate([x_prompt.reshape(mp, d), x_sample.reshape(ms, d)], axis=0)
    row_cond = jnp.concatenate([jnp.zeros((mp,), jnp.int32),
                                1 + jnp.arange(ms, dtype=jnp.int32) // ts])
    rsel = (row_cond[:, None] == jnp.arange(MOD_ROWS, dtype=jnp.int32)[None, :]).astype(BF16)
    c_all = jnp.concatenate([c_prompt, c_sample,
                             jnp.zeros((MOD_ROWS - bp - bs, d), F32)], axis=0)
    mod = _modulation(c_all, w_ada, b_ada)

    zeros = lambda *s: jnp.zeros(s, F32)
    pad_m = lambda mm: jnp.pad(mm, ((0, 0), (0, LANES - mm.shape[1])))[:, None, :]
    outs_p = [[] for _ in range(6)]
    outs_s = [[] for _ in range(6)]

    w_br_a16, w_br_b16, w_br_c16 = w_br_a.astype(BF16), w_br_b.astype(BF16), w_br_c.astype(BF16)
    w_out16 = w_out.astype(BF16)

    for l in range(depth):
        mod_l = mod[l:l + 1]
        h = _norm_mod(x, rsel, norm_mix[l], mod_l, 0, 1)
        p = _matmul(h, pack_w_in(l), F32, TM_BIG, TN_IN)

        bias_if = jnp.pad(b_gate_a[l].reshape(1, 2 * nh_a), ((0, 0), (0, LANES - 2 * nh_a)))
        w_gk2p = jnp.pad(w_gk2[l], ((lr_lane, LANES - lr_lane - gate_rank), (0, 0)))
        groups = (
            dict(row0=0, B=bp, T=tp, La=64, Lg=64,
                 conv0=zeros(bp, CONV_W - 1, 2 * qk_a), c0=zeros(bp, nh_a, dv_a, dk_a),
                 n0=zeros(bp, nh_a, dk_a), m0=zeros(bp, 1, LANES),
                 sb0=zeros(bp, nh_b, dv_b, dk_b), sc0=zeros(bp, nh_c, dv_c, dk_c)),
            dict(row0=mp, B=bs, T=ts, La=ts, Lg=ts,
                 conv0=state_a_conv[l], c0=state_a_C[l], n0=state_a_n[l],
                 m0=pad_m(state_a_m[l]),
                 sb0=jnp.swapaxes(state_b_S[l], -1, -2), sc0=jnp.swapaxes(state_c_S[l], -1, -2)),
        )
        o_a = o_b = o_c = None
        for g, outs in zip(groups, (outs_p, outs_s)):
            o_a, a_c, a_n, a_m, a_conv = _mlstm(
                p, (col["a_qk"], col["a_v"], col["a_o"], col["a_if"]),
                g["conv0"], g["c0"], g["n0"], g["m0"], conv_w[l], conv_b[l].reshape(1, -1),
                bias_if, hn_a[l].reshape(1, -1), o_a,
                row0=g["row0"], B=g["B"], T=g["T"], L=g["La"], NH=nh_a, DK=dk_a, DV=dv_a)
            o_b, b_st = _gla_b(
                p, (col["b_q"], col["b_k"], col["b_v"], col["b_g"], col["b_lr"]),
                g["sb0"], w_gk2p, b_gk[l].reshape(1, -1), hn_b[l].reshape(1, -1), o_b,
                row0=g["row0"], B=g["B"], T=g["T"], L=g["Lg"], NH=nh_b, DK=dk_b, DV=dv_b, HP=4)
            o_c, c_st = _gla_c(
                p, (col["c_q"], col["c_f"], col["c_i"], col["c_g"]),
                g["sc0"], lb_logits, hn_c[l].reshape(1, -1), o_c,
                layer=l, row0=g["row0"], B=g["B"], T=g["T"], L=g["Lg"],
                NH=nh_c, DK=dk_c, DV=dv_c, HP=8)
            for lst, val in zip(outs, (a_c, a_n, a_m[:, 0, :nh_a], a_conv,
                                       jnp.swapaxes(b_st, -1, -2), jnp.swapaxes(c_st, -1, -2))):
                lst.append(val)

        merged = _merge(o_a, o_b, o_c, w_br_a16, w_br_b16, w_br_c16, l, p, col["gates"])
        x = _out_proj(merged, w_out16, l, x, rsel, mod_l, 2)

        w_r = jnp.pad(jnp.concatenate([w_rg[l], w_re[l]], axis=1),
                      ((0, 0), (0, LANES - N_GROUPS - N_EXPERTS)))
        b_r = jnp.pad(jnp.concatenate([b_rg[l], b_re[l]]),
                      (0, LANES - N_GROUPS - N_EXPERTS)).reshape(1, LANES)
        h2, logits = _norm_mod_router(x, rsel, norm_moe[l], mod_l, 3, 4, w_r, b_r)
        eid, wts = _route(logits)
        rows, roww, blk_e, n_act, slot = _dispatch(eid, wts)
        yb = _expert_mlp(h2[rows], roww, blk_e, n_act, w_exp1, w_exp3, w_exp2, l)
        x = _gated_add(x, yb[slot[:, 0]], yb[slot[:, 1]], rsel, mod_l, 5)

    y_prompt = _final_norm(x, norm_final, 0, mp, 512).reshape(bp, tp, d)
    y_sample = _final_norm(x, norm_final, mp, ms, ms).reshape(bs, ts, d)
    stack = lambda lst: jnp.stack(lst)
    return (y_prompt, y_sample,
            stack(outs_p[0]), stack(outs_p[1]), stack(outs_p[2]), stack(outs_p[3]),
            stack(outs_p[4]), stack(outs_p[5]),
            stack(outs_s[0]), stack(outs_s[1]), stack(outs_s[2]), stack(outs_s[3]),
            stack(outs_s[4]), stack(outs_s[5]))
```

```python
import functools

import jax
import jax.numpy as jnp
import numpy as np
from jax import lax
from jax.experimental import pallas as pl
from jax.experimental.pallas import tpu as pltpu

F32 = jnp.float32
BF16 = jnp.bfloat16

NORM_EPS = 1e-6
NEG_BIG = -1e30
F_TINY = 1e-30
GLA_NORM = 16.0
CONV_W = 4
TOP_K = 2
N_GROUPS = 4
EXPERTS_PER_GROUP = 8
N_EXPERTS = N_GROUPS * EXPERTS_PER_GROUP
N_BRANCH = 3

LANES = 128
MOD_ROWS = 16
VMEM_LIMIT = 56 * 1024 * 1024

TM_BIG = 1040
TM_SMALL = 520
MOE_ROWS = 256
W_CHUNKS = 4
TN_PROJ = 1024
SUB = 16
SAFE_LOG = 60.0


def _cparams(sem):
    return pltpu.CompilerParams(dimension_semantics=sem, vmem_limit_bytes=VMEM_LIMIT)


def _dot(a, b):
    return jnp.dot(a, b, preferred_element_type=F32)


def _dot_nt(a, b):
    return lax.dot_general(a, b, (((1,), (1,)), ((), ())), preferred_element_type=F32)


def _dot_tn(a, b):
    return lax.dot_general(a, b, (((0,), (0,)), ((), ())), preferred_element_type=F32)


def _split3(x):
    hi = x.astype(BF16)
    r = x - hi.astype(F32)
    mid = r.astype(BF16)
    lo = (r - mid.astype(F32)).astype(BF16)
    return hi, mid, lo


def _dot_sel(sel, x):
    hi, mid, lo = _split3(x)
    return _dot(sel, hi) + _dot(sel, mid) + _dot(sel, lo)


def _dot_nt_sel(sel, x):
    hi, mid, lo = _split3(x)
    return _dot_nt(sel, hi) + _dot_nt(sel, mid) + _dot_nt(sel, lo)


def _dot_nt_f32(a, b):
    a_hi = a.astype(BF16)
    a_lo = (a - a_hi.astype(F32)).astype(BF16)
    b_hi = b.astype(BF16)
    b_lo = (b - b_hi.astype(F32)).astype(BF16)
    return _dot_nt(a_hi, b_hi) + _dot_nt(a_hi, b_lo) + _dot_nt(a_lo, b_hi)


def _r16(x):
    return x.astype(BF16).astype(F32)


def _pack_bf16_pairs(xb):
    n = xb.shape[1] // 2
    bits = pltpu.bitcast(xb.astype(F32), jnp.int32)
    return lax.shift_right_logical(bits[:, :n], 16) | bits[:, n:]


def _unpack_bf16_pairs(w):
    lo = pltpu.bitcast(lax.shift_left(w, 16), F32).astype(BF16)
    hi = pltpu.bitcast(w & jnp.int32(-65536), F32).astype(BF16)
    return lo, hi


def _sigmoid(x):
    return jax.nn.sigmoid(x)


def _silu(x):
    return x * jax.nn.sigmoid(x)


def _log_sigmoid(x):
    return jnp.minimum(x, 0.0) - jnp.log1p(jnp.exp(-jnp.abs(x)))


def _rms(x):
    return x * lax.rsqrt(jnp.mean(x * x, axis=-1, keepdims=True) + NORM_EPS)


def _tril(n):
    r = lax.broadcasted_iota(jnp.int32, (n, n), 0)
    c = lax.broadcasted_iota(jnp.int32, (n, n), 1)
    return jnp.where(r >= c, 1.0, 0.0).astype(BF16)


def _eye(n):
    r = lax.broadcasted_iota(jnp.int32, (n, n), 0)
    c = lax.broadcasted_iota(jnp.int32, (n, n), 1)
    return jnp.where(r == c, 1.0, 0.0).astype(BF16)


def _mod_kernel(c_ref, w_ref, b_ref, o_ref):
    c = c_ref[...]
    o_ref[0] = _dot(_silu(c).astype(BF16), w_ref[0].astype(BF16)) + b_ref[0]


def _modulation(c_all, w_ada, b_ada):
    depth, d, n = w_ada.shape
    tn = 1024
    return pl.pallas_call(
        _mod_kernel,
        grid=(depth, n // tn),
        in_specs=[pl.BlockSpec((MOD_ROWS, d), lambda l, j: (0, 0)),
                  pl.BlockSpec((1, d, tn), lambda l, j: (l, 0, j)),
                  pl.BlockSpec((1, 1, tn), lambda l, j: (l, 0, j))],
        out_specs=pl.BlockSpec((1, MOD_ROWS, tn), lambda l, j: (l, 0, j)),
        out_shape=jax.ShapeDtypeStruct((depth, MOD_ROWS, n), F32),
        compiler_params=_cparams(("arbitrary", "arbitrary")),
        name="modulation",
    )(c_all, w_ada, b_ada.reshape(depth, 1, n))


def _row_mod(mixed, r_ref, m_ref):
    if mixed:
        return _dot_sel(r_ref[...], m_ref[0])
    return m_ref[0, 0:1, :]


def _norm_mod_kernel(x_ref, r_ref, g_ref, sh_ref, sc_ref, o_ref):
    last = pl.num_programs(0) - 1

    def body(mixed):
        y = _rms(x_ref[...]) * g_ref[...]
        o_ref[...] = (y * (1.0 + _row_mod(mixed, r_ref, sc_ref)) + _row_mod(mixed, r_ref, sh_ref)
                      ).astype(o_ref.dtype)

    pl.when(pl.program_id(0) != last)(lambda: body(False))
    pl.when(pl.program_id(0) == last)(lambda: body(True))


def _norm_mod(x, rsel, gain, mod_l, sh_idx, sc_idx):
    m, d = x.shape
    tm = TM_SMALL
    return pl.pallas_call(
        _norm_mod_kernel,
        grid=(m // tm,),
        in_specs=[pl.BlockSpec((tm, d), lambda i: (i, 0)),
                  pl.BlockSpec((tm, MOD_ROWS), lambda i: (i, 0)),
                  pl.BlockSpec((1, d), lambda i: (0, 0)),
                  pl.BlockSpec((1, MOD_ROWS, d), lambda i: (0, 0, sh_idx)),
                  pl.BlockSpec((1, MOD_ROWS, d), lambda i: (0, 0, sc_idx))],
        out_specs=pl.BlockSpec((tm, d), lambda i: (i, 0)),
        out_shape=jax.ShapeDtypeStruct((m, d), BF16),
        compiler_params=_cparams(("arbitrary",)),
        name="norm_mod",
    )(x, rsel, gain.reshape(1, d), mod_l, mod_l)


def _norm_mod_router_kernel(x_ref, r_ref, g_ref, sh_ref, sc_ref, wr_ref, br_ref, o_ref, lg_ref):
    last = pl.num_programs(0) - 1

    def body(mixed):
        y = _rms(x_ref[...]) * g_ref[...]
        h = y * (1.0 + _row_mod(mixed, r_ref, sc_ref)) + _row_mod(mixed, r_ref, sh_ref)
        hb = h.astype(BF16)
        o_ref[...] = _pack_bf16_pairs(hb)
        lg_ref[...] = _dot(hb, wr_ref[...].astype(BF16)) + br_ref[...]

    pl.when(pl.program_id(0) != last)(lambda: body(False))
    pl.when(pl.program_id(0) == last)(lambda: body(True))


def _norm_mod_router(x, rsel, gain, mod_l, sh_idx, sc_idx, w_r, b_r):
    m, d = x.shape
    tm = TM_SMALL
    return pl.pallas_call(
        _norm_mod_router_kernel,
        grid=(m // tm,),
        in_specs=[pl.BlockSpec((tm, d), lambda i: (i, 0)),
                  pl.BlockSpec((tm, MOD_ROWS), lambda i: (i, 0)),
                  pl.BlockSpec((1, d), lambda i: (0, 0)),
                  pl.BlockSpec((1, MOD_ROWS, d), lambda i: (0, 0, sh_idx)),
                  pl.BlockSpec((1, MOD_ROWS, d), lambda i: (0, 0, sc_idx)),
                  pl.BlockSpec((d, LANES), lambda i: (0, 0)),
                  pl.BlockSpec((1, LANES), lambda i: (0, 0))],
        out_specs=[pl.BlockSpec((tm, d // 2), lambda i: (i, 0)),
                   pl.BlockSpec((tm, LANES), lambda i: (i, 0))],
        out_shape=[jax.ShapeDtypeStruct((m, d // 2), jnp.int32),
                   jax.ShapeDtypeStruct((m, LANES), F32)],
        compiler_params=_cparams(("arbitrary",)),
        name="norm_mod_router",
    )(x, rsel, gain.reshape(1, d), mod_l, mod_l, w_r, b_r)


def _final_norm_kernel(x_ref, g_ref, o_ref):
    o_ref[...] = _rms(x_ref[...]) * g_ref[...]


def _final_norm(x, gain, row0, rows, tm):
    d = x.shape[1]
    blk0 = row0 // tm
    return pl.pallas_call(
        _final_norm_kernel,
        grid=(rows // tm,),
        in_specs=[pl.BlockSpec((tm, d), lambda i: (blk0 + i, 0)),
                  pl.BlockSpec((1, d), lambda i: (0, 0))],
        out_specs=pl.BlockSpec((tm, d), lambda i: (i, 0)),
        out_shape=jax.ShapeDtypeStruct((rows, d), F32),
        compiler_params=_cparams(("arbitrary",)),
        name="final_norm",
    )(x, gain.reshape(1, d))


def _matmul_kernel(a_ref, b_ref, o_ref):
    o_ref[...] = _dot(a_ref[...], b_ref[...]).astype(o_ref.dtype)


def _matmul(a, b, out_dtype, tm, tn):
    m, k = a.shape
    n = b.shape[1]
    return pl.pallas_call(
        _matmul_kernel,
        grid=(m // tm, n // tn),
        in_specs=[pl.BlockSpec((tm, k), lambda i, j: (i, 0)),
                  pl.BlockSpec((k, tn), lambda i, j: (0, j))],
        out_specs=pl.BlockSpec((tm, tn), lambda i, j: (i, j)),
        out_shape=jax.ShapeDtypeStruct((m, n), out_dtype),
        compiler_params=_cparams(("arbitrary", "arbitrary")),
        name="in_proj",
    )(a, b)


def _pack_kernel(a_ref, b_ref, o_ref, *, shift):
    if shift == 0:
        o_ref[...] = a_ref[...].astype(BF16)
    else:
        tn = a_ref.shape[1]
        cat = jnp.concatenate([a_ref[...], b_ref[...]], axis=1)
        o_ref[...] = cat[:, shift:shift + tn].astype(BF16)


def _pack_cols(w, layer, start, width):
    k = w.shape[1]
    kb, tn = 1024, TN_PROJ
    shift = start % LANES
    base = start - shift
    assert base % tn == 0 and width % tn == 0 and k % kb == 0
    return pl.pallas_call(
        functools.partial(_pack_kernel, shift=shift),
        grid=(k // kb, width // tn),
        in_specs=[pl.BlockSpec((None, kb, tn), lambda i, j: (layer, i, base // tn + j)),
                  pl.BlockSpec((None, kb, LANES),
                               lambda i, j: (layer, i, (base + (j + 1) * tn) // LANES))],
        out_specs=pl.BlockSpec((kb, tn), lambda i, j: (i, j)),
        out_shape=jax.ShapeDtypeStruct((k, width), BF16),
        compiler_params=_cparams(("arbitrary", "arbitrary")),
        name="pack_w_in",
    )(w, w)


def _merge_kernel(oa_ref, ob_ref, oc_ref, wa_ref, wb_ref, wc_ref, ga_ref, gb_ref, gc_ref, o_ref):
    acc = _sigmoid(ga_ref[...]) * _dot(oa_ref[...], wa_ref[...])
    acc = acc + _sigmoid(gb_ref[...]) * _dot(ob_ref[...], wb_ref[...])
    acc = acc + _sigmoid(gc_ref[...]) * _dot(oc_ref[...], wc_ref[...])
    o_ref[...] = acc.astype(o_ref.dtype)


def _merge(o_a, o_b, o_c, w_a, w_b, w_c, layer, p, gate_col0):
    m, kw = o_a.shape
    d = w_a.shape[2]
    tm, tn = TM_SMALL, 512
    g0 = gate_col0 // tn
    gstride = d // tn
    o_spec = pl.BlockSpec((tm, kw), lambda j, i: (i, 0))
    w_spec = pl.BlockSpec((None, kw, tn), lambda j, i: (layer, 0, j))

    def g_spec(br):
        return pl.BlockSpec((tm, tn), lambda j, i: (i, g0 + br * gstride + j))

    return pl.pallas_call(
        _merge_kernel,
        grid=(d // tn, m // tm),
        in_specs=[o_spec, o_spec, o_spec, w_spec, w_spec, w_spec, g_spec(0), g_spec(1), g_spec(2)],
        out_specs=pl.BlockSpec((tm, tn), lambda j, i: (i, j)),
        out_shape=jax.ShapeDtypeStruct((m, d), BF16),
        compiler_params=_cparams(("arbitrary", "arbitrary")),
        name="merge",
    )(o_a, o_b, o_c, w_a, w_b, w_c, p, p, p)


def _out_proj_kernel(a_ref, w_ref, x_ref, r_ref, g_ref, o_ref):
    last = pl.num_programs(0) - 1
    acc = _dot(a_ref[...], w_ref[...])

    def body(mixed):
        o_ref[...] = x_ref[...] + _row_mod(mixed, r_ref, g_ref) * acc

    pl.when(pl.program_id(0) != last)(lambda: body(False))
    pl.when(pl.program_id(0) == last)(lambda: body(True))


def _out_proj(a, w, layer, x, rsel, mod_l, g_idx):
    m, k = a.shape
    d = w.shape[2]
    tm, tn = TM_BIG, 512
    nj = d // tn
    return pl.pallas_call(
        _out_proj_kernel,
        grid=(m // tm, nj),
        in_specs=[pl.BlockSpec((tm, k), lambda i, j: (i, 0)),
                  pl.BlockSpec((None, k, tn), lambda i, j: (layer, 0, j)),
                  pl.BlockSpec((tm, tn), lambda i, j: (i, j)),
                  pl.BlockSpec((tm, MOD_ROWS), lambda i, j: (i, 0)),
                  pl.BlockSpec((1, MOD_ROWS, tn), lambda i, j: (0, 0, g_idx * nj + j))],
        out_specs=pl.BlockSpec((tm, tn), lambda i, j: (i, j)),
        out_shape=jax.ShapeDtypeStruct((m, d), F32),
        compiler_params=_cparams(("arbitrary", "arbitrary")),
        name="out_proj",
    )(a, w, x, rsel, mod_l)


def _gated_add_kernel(x_ref, y0_ref, y1_ref, r_ref, g_ref, o_ref):
    last = pl.num_programs(0) - 1

    def body(mixed):
        o_ref[...] = x_ref[...] + _row_mod(mixed, r_ref, g_ref) * (y0_ref[...] + y1_ref[...])

    pl.when(pl.program_id(0) != last)(lambda: body(False))
    pl.when(pl.program_id(0) == last)(lambda: body(True))


def _gated_add(x, y0, y1, rsel, mod_l, g_idx):
    m, d = x.shape
    tm, tn = TM_SMALL, 1024
    nj = d // tn
    spec = pl.BlockSpec((tm, tn), lambda i, j: (i, j))
    return pl.pallas_call(
        _gated_add_kernel,
        grid=(m // tm, nj),
        in_specs=[spec, spec, spec,
                  pl.BlockSpec((tm, MOD_ROWS), lambda i, j: (i, 0)),
                  pl.BlockSpec((1, MOD_ROWS, tn), lambda i, j: (0, 0, g_idx * nj + j))],
        out_specs=spec,
        out_shape=jax.ShapeDtypeStruct((m, d), F32),
        compiler_params=_cparams(("arbitrary", "arbitrary")),
        name="moe_combine",
    )(x, y0, y1, rsel, mod_l)


def _alias_prev(kern, in_specs, args, o_prev):
    if o_prev is None:
        return kern, in_specs, args, {}
    n = len(args)

    def wrapped(*refs):
        return kern(*refs[:n], *refs[n + 1:])

    return wrapped, in_specs + [pl.BlockSpec(memory_space=pl.ANY)], args + [o_prev], {n: 0}


def _mlstm_kernel(qp_ref, kp_ref, v_ref, og_ref, if_ref, conv0_ref, c0_ref, n0_ref, m0_ref,
                  cw_ref, cb_ref, bif_ref, hn_ref,
                  o_ref, cout_ref, nout_ref, mout_ref, convout_ref,
                  c_s, n_s, m_s, ubuf, q_s, k_s, *, L, NH, DK, DV):
    ci = pl.program_id(1)
    QK = NH * DK

    @pl.when(ci == 0)
    def _():
        c_s[...] = c0_ref[0]
        n_s[...] = n0_ref[0]
        m_s[...] = m0_ref[0]
        ubuf[8 - (CONV_W - 1):8, :] = conv0_ref[0]

    ubuf[8:8 + L, 0:QK] = qp_ref[...]
    ubuf[8:8 + L, QK:2 * QK] = kp_ref[...]
    y = cb_ref[...]
    for j in range(CONV_W):
        y = y + ubuf[8 - (CONV_W - 1) + j:8 - (CONV_W - 1) + j + L, :] * cw_ref[j:j + 1, :]
    ubuf[0:8, :] = ubuf[L:L + 8, :]
    qk = _silu(y)
    q_s[...] = qk[:, 0:QK] * (DK ** -0.5)
    k_s[...] = qk[:, QK:2 * QK]

    ifv = if_ref[...] + bif_ref[...]
    b_all = _dot_sel(_tril(L), _log_sigmoid(ifv))
    eye = _eye(LANES)
    ig_t = _dot_nt_sel(eye, ifv)
    b_t = _dot_nt_sel(eye, b_all)
    row = lax.broadcasted_iota(jnp.int32, (L, L), 0)
    col = lax.broadcasted_iota(jnp.int32, (L, L), 1)
    causal = row >= col
    lane = lax.broadcasted_iota(jnp.int32, (1, LANES), 1)
    m_old = m_s[...]
    m_new = m_old

    for h in range(NH):
        b_c = b_all[:, NH + h:NH + h + 1]
        ig_c = ifv[:, h:h + 1]
        b_r = b_t[NH + h:NH + h + 1, :]
        ig_r = ig_t[h:h + 1, :]
        m_prev = m_old[:, h:h + 1]
        log_d = jnp.where(causal, b_c - b_r + ig_r, NEG_BIG)
        inter = b_c + m_prev
        mt = jnp.maximum(jnp.max(log_d, axis=-1, keepdims=True), inter)
        dm = jnp.exp(log_d - mt)
        sc = jnp.exp(inter - mt)
        qh = q_s[:, h * DK:(h + 1) * DK]
        kh = k_s[:, h * DK:(h + 1) * DK]
        vh = v_ref[:, h * DV:(h + 1) * DV]
        qb, kb, vb = qh.astype(BF16), kh.astype(BF16), vh.astype(BF16)
        c_h = c_s[h]
        n_h = n_s[h:h + 1, :]
        s = _dot_nt(qb, kb) * dm
        num = sc * _dot_nt(qb, c_h.astype(BF16)) + _dot(s.astype(BF16), vb)
        den = (sc * jnp.sum(qb.astype(F32) * _r16(n_h), axis=-1, keepdims=True)
               + jnp.sum(s, axis=-1, keepdims=True))
        hc = num / jnp.maximum(jnp.abs(den), jnp.exp(-mt))
        mt_l = mt[L - 1:L, :]
        w_l = jnp.exp((b_c[L - 1:L, :] - b_c) + ig_c - mt_l)
        s_l = sc[L - 1:L, :]
        c_s[h] = s_l * c_h + _dot_tn(vb, (kh * w_l).astype(BF16))
        n_s[h:h + 1, :] = s_l * n_h + jnp.sum(_r16(w_l) * kb.astype(F32), axis=0, keepdims=True)
        m_new = jnp.where(lane == h, mt_l, m_new)
        out = _rms(hc) * hn_ref[:, h * DV:(h + 1) * DV] * _sigmoid(og_ref[:, h * DV:(h + 1) * DV])
        o_ref[:, h * DV:(h + 1) * DV] = out.astype(o_ref.dtype)

    m_s[...] = m_new

    @pl.when(ci == pl.num_programs(1) - 1)
    def _():
        cout_ref[0] = c_s[...]
        nout_ref[0] = n_s[...]
        mout_ref[0] = m_s[...]
        convout_ref[0] = ubuf[8 - (CONV_W - 1):8, :]


def _mlstm(srcs, conv0, c0, n0, m0, conv_w, conv_b, bias_if, hn, o_prev,
           *, row0, B, T, L, NH, DK, DV):
    QK, W = NH * DK, NH * DV
    nC = T // L
    rb0 = row0 // L
    (a_q, c_q), (a_k, c_k), (a_v, c_v), (a_o, c_o), (a_if, c_if) = srcs

    def rows(b, c):
        return rb0 + b * nC + c

    kern = functools.partial(_mlstm_kernel, L=L, NH=NH, DK=DK, DV=DV)
    in_specs = [pl.BlockSpec((L, QK), lambda b, c: (rows(b, c), c_q // QK)),
                  pl.BlockSpec((L, QK), lambda b, c: (rows(b, c), c_k // QK)),
                  pl.BlockSpec((L, W), lambda b, c: (rows(b, c), c_v // W)),
                  pl.BlockSpec((L, W), lambda b, c: (rows(b, c), c_o // W)),
                  pl.BlockSpec((L, LANES), lambda b, c: (rows(b, c), c_if // LANES)),
                  pl.BlockSpec((1, CONV_W - 1, 2 * QK), lambda b, c: (b, 0, 0)),
                  pl.BlockSpec((1, NH, DV, DK), lambda b, c: (b, 0, 0, 0)),
                  pl.BlockSpec((1, NH, DK), lambda b, c: (b, 0, 0)),
                  pl.BlockSpec((1, 1, LANES), lambda b, c: (b, 0, 0)),
                  pl.BlockSpec((CONV_W, 2 * QK), lambda b, c: (0, 0)),
                  pl.BlockSpec((1, 2 * QK), lambda b, c: (0, 0)),
                  pl.BlockSpec((1, LANES), lambda b, c: (0, 0)),
                  pl.BlockSpec((1, W), lambda b, c: (0, 0))]
    args = [a_q, a_k, a_v, a_o, a_if, conv0, c0, n0, m0, conv_w, conv_b, bias_if, hn]
    kern, in_specs, args, aliases = _alias_prev(kern, in_specs, args, o_prev)
    return pl.pallas_call(
        kern,
        grid=(B, nC),
        in_specs=in_specs,
        out_specs=[pl.BlockSpec((L, W), lambda b, c: (rows(b, c), 0)),
                   pl.BlockSpec((1, NH, DV, DK), lambda b, c: (b, 0, 0, 0)),
                   pl.BlockSpec((1, NH, DK), lambda b, c: (b, 0, 0)),
                   pl.BlockSpec((1, 1, LANES), lambda b, c: (b, 0, 0)),
                   pl.BlockSpec((1, CONV_W - 1, 2 * QK), lambda b, c: (b, 0, 0))],
        out_shape=[jax.ShapeDtypeStruct((a_q.shape[0], W), BF16),
                   jax.ShapeDtypeStruct((B, NH, DV, DK), F32),
                   jax.ShapeDtypeStruct((B, NH, DK), F32),
                   jax.ShapeDtypeStruct((B, 1, LANES), F32),
                   jax.ShapeDtypeStruct((B, CONV_W - 1, 2 * QK), F32)],
        scratch_shapes=[pltpu.VMEM((NH, DV, DK), F32),
                        pltpu.VMEM((NH, DK), F32),
                        pltpu.VMEM((1, LANES), F32),
                        pltpu.VMEM((L + 8, 2 * QK), F32),
                        pltpu.VMEM((L, QK), F32),
                        pltpu.VMEM((L, QK), F32)],
        input_output_aliases=aliases,
        compiler_params=_cparams(("arbitrary", "arbitrary")),
        name="mixer_mlstm",
    )(*args)


def _gla_intra_exact(q, k, b, L, c):
    dk = q.shape[1]
    nsub = L // c
    q3 = q.reshape(nsub, c, dk)
    k3 = k.reshape(nsub, c, dk)
    b3 = b.reshape(nsub, c, dk)
    t_idx = lax.broadcasted_iota(jnp.int32, (1, c, 1), 1)
    s_idx = lax.broadcasted_iota(jnp.int32, (1, 1, c), 2)
    a_diag = jnp.zeros((nsub, c, c), F32)
    for s in range(c):
        arg = jnp.where(t_idx >= s, b3 - b3[:, s:s + 1, :], NEG_BIG)
        col_s = jnp.sum(q3 * k3[:, s:s + 1, :] * jnp.exp(arg), axis=-1, keepdims=True)
        a_diag = jnp.where(s_idx == s, col_s, a_diag)
    a_diag = a_diag.reshape(L, c)
    if nsub == 1:
        return a_diag
    row = lax.broadcasted_iota(jnp.int32, (L, L), 0)
    col = lax.broadcasted_iota(jnp.int32, (L, L), 1)
    rep_r = lax.broadcasted_iota(jnp.int32, (c, L), 0)
    rep_c = lax.broadcasted_iota(jnp.int32, (c, L), 1)
    rep = jnp.where((rep_c & (c - 1)) == rep_r, 1.0, 0.0).astype(BF16)
    a = jnp.where((row & -c) == (col & -c), _dot(a_diag.astype(BF16), rep), 0.0)
    blocks = [jnp.zeros((c, L), F32)]
    for i in range(1, nsub):
        r_i = b[i * c - 1:i * c, :]
        q_i = q[i * c:(i + 1) * c, :] * jnp.exp(b[i * c:(i + 1) * c, :] - r_i)
        k_i = k * jnp.exp(jnp.minimum(r_i - b, 0.0))
        blocks.append(_dot_nt_f32(q_i, k_i))
    return a + jnp.where((col & -c) < (row & -c), jnp.concatenate(blocks, axis=0), 0.0)


def _gla_intra_fact(q, k, b, L, c):
    nsub = L // c
    row = lax.broadcasted_iota(jnp.int32, (L, L), 0)
    col = lax.broadcasted_iota(jnp.int32, (L, L), 1)
    blocks = []
    for i in range(nsub):
        b_i = b[i * c:(i + 1) * c, :]
        if i == 0:
            q_i = q[0:c, :] * jnp.exp(b_i)
            k_i = k * jnp.exp(jnp.minimum(-b, SAFE_LOG))
        else:
            r_i = b[i * c - 1:i * c, :]
            q_i = q[i * c:(i + 1) * c, :] * jnp.exp(b_i - r_i)
            k_i = k * jnp.exp(jnp.minimum(r_i - b, SAFE_LOG))
        blocks.append(_dot_nt_f32(q_i, k_i))
    a = blocks[0] if nsub == 1 else jnp.concatenate(blocks, axis=0)
    return jnp.where(col <= row, a, 0.0)


def _gla_cum_decay(g, L):
    c = min(SUB, L)
    nsub = L // c
    b = _dot_sel(_tril(L), g)
    drops = [b[c - 1:c, :]] + [b[(i + 1) * c - 1:(i + 1) * c, :] - b[i * c - 1:i * c, :]
                               for i in range(1, nsub)]
    return b, jnp.min(drops[0] if nsub == 1 else jnp.concatenate(drops, axis=0))


def _either(pred, body):
    pl.when(pred)(lambda: body(True))
    pl.when(jnp.logical_not(pred))(lambda: body(False))


def _gla_chunk(q, k, v, b, st, L, fact):
    c = min(SUB, L)
    b_l = b[L - 1:L, :]
    o = _dot_nt((q * jnp.exp(b)).astype(BF16), st.astype(BF16))
    a = _gla_intra_fact(q, k, b, L, c) if fact else _gla_intra_exact(q, k, b, L, c)
    vb = v.astype(BF16)
    o = o + _dot(a.astype(BF16), vb)
    st_new = st * jnp.exp(b_l) + _dot_tn(vb, (k * jnp.exp(b_l - b)).astype(BF16))
    return o, st_new


def _gla_b_kernel(q_ref, k_ref, v_ref, gt_ref, lr_ref, s0_ref, w2_ref, bgk_ref, hn_ref,
                  o_ref, sout_ref, st_s, *, L, HP, DK, DV):
    ci = pl.program_id(2)

    @pl.when(ci == 0)
    def _():
        st_s[...] = s0_ref[0]

    z = _dot(lr_ref[...].astype(BF16), w2_ref[...].astype(BF16)) + bgk_ref[...]
    b, worst = _gla_cum_decay(_log_sigmoid(z) / GLA_NORM, L)

    def body(fact):
        for h in range(HP):
            sl = slice(h * DK, (h + 1) * DK)
            sv = slice(h * DV, (h + 1) * DV)
            o, st_new = _gla_chunk(q_ref[:, sl] * (DK ** -0.5), k_ref[:, sl], v_ref[:, sv],
                                   b[:, sl], st_s[h], L, fact)
            st_s[h] = st_new
            o_ref[:, sv] = (_rms(o) * hn_ref[:, sv] * _silu(gt_ref[:, sv])).astype(o_ref.dtype)

    _either(worst > -SAFE_LOG, body)

    @pl.when(ci == pl.num_programs(2) - 1)
    def _():
        sout_ref[0] = st_s[...]


def _gla_b(srcs, s0t, w_gk2p, b_gk, hn, o_prev, *, row0, B, T, L, NH, DK, DV, HP):
    nC = T // L
    rb0 = row0 // L
    (a_q, c_q), (a_k, c_k), (a_v, c_v), (a_g, c_g), (a_lr, c_lr) = srcs
    wq, wv = HP * DK, HP * DV

    def rows(b, c):
        return rb0 + b * nC + c

    kern = functools.partial(_gla_b_kernel, L=L, HP=HP, DK=DK, DV=DV)
    in_specs = [pl.BlockSpec((L, wq), lambda b, g, c: (rows(b, c), c_q // wq + g)),
                pl.BlockSpec((L, wq), lambda b, g, c: (rows(b, c), c_k // wq + g)),
                pl.BlockSpec((L, wv), lambda b, g, c: (rows(b, c), c_v // wv + g)),
                pl.BlockSpec((L, wv), lambda b, g, c: (rows(b, c), c_g // wv + g)),
                pl.BlockSpec((L, LANES), lambda b, g, c: (rows(b, c), c_lr // LANES)),
                pl.BlockSpec((1, HP, DV, DK), lambda b, g, c: (b, g, 0, 0)),
                pl.BlockSpec((LANES, wq), lambda b, g, c: (0, g)),
                pl.BlockSpec((1, wq), lambda b, g, c: (0, g)),
                pl.BlockSpec((1, wv), lambda b, g, c: (0, g))]
    args = [a_q, a_k, a_v, a_g, a_lr, s0t, w_gk2p, b_gk, hn]
    kern, in_specs, args, aliases = _alias_prev(kern, in_specs, args, o_prev)
    return pl.pallas_call(
        kern,
        grid=(B, NH // HP, nC),
        in_specs=in_specs,
        out_specs=[pl.BlockSpec((L, wv), lambda b, g, c: (rows(b, c), g)),
                   pl.BlockSpec((1, HP, DV, DK), lambda b, g, c: (b, g, 0, 0))],
        out_shape=[jax.ShapeDtypeStruct((a_q.shape[0], NH * DV), BF16),
                   jax.ShapeDtypeStruct((B, NH, DV, DK), F32)],
        scratch_shapes=[pltpu.VMEM((HP, DV, DK), F32)],
        input_output_aliases=aliases,
        compiler_params=_cparams(("arbitrary", "arbitrary", "arbitrary")),
        name="mixer_gla",
    )(*args)


def _gla_c_kernel(q_ref, f_ref, i_ref, gt_ref, s0_ref, lbl_ref, hn_ref,
                  o_ref, sout_ref, st_s, *, L, HP, DK, DV, layer):
    ci = pl.program_id(2)

    @pl.when(ci == 0)
    def _():
        st_s[...] = s0_ref[0]

    lbl = lbl_ref[...]
    e = jnp.exp(lbl - jnp.max(lbl, axis=0, keepdims=True))
    sm = e / jnp.sum(e, axis=0, keepdims=True)
    lb = jnp.sum(sm[0:layer + 1, :], axis=0, keepdims=True) - sm[0:1, :]

    f = lb + (1.0 - lb) * _sigmoid(f_ref[...])
    b, worst = _gla_cum_decay(jnp.log(jnp.maximum(f, F_TINY)), L)

    def body(fact):
        for h in range(HP):
            sl = slice(h * DK, (h + 1) * DK)
            sv = slice(h * DV, (h + 1) * DV)
            o, st_new = _gla_chunk(_silu(q_ref[:, sl]), 1.0 - f[:, sl], i_ref[:, sv], b[:, sl],
                                   st_s[h], L, fact)
            st_s[h] = st_new
            o_ref[:, sv] = (_rms(o) * hn_ref[:, sv] * _silu(gt_ref[:, sv])).astype(o_ref.dtype)

    _either(worst > -SAFE_LOG, body)

    @pl.when(ci == pl.num_programs(2) - 1)
    def _():
        sout_ref[0] = st_s[...]


def _gla_c(srcs, s0t, lb_logits, hn, o_prev, *, layer, row0, B, T, L, NH, DK, DV, HP):
    nC = T // L
    rb0 = row0 // L
    (a_q, c_q), (a_f, c_f), (a_i, c_i), (a_g, c_g) = srcs
    wq, wv = HP * DK, HP * DV

    def rows(b, c):
        return rb0 + b * nC + c

    kern = functools.partial(_gla_c_kernel, L=L, HP=HP, DK=DK, DV=DV, layer=layer)
    in_specs = [pl.BlockSpec((L, wq), lambda b, g, c: (rows(b, c), c_q // wq + g)),
                pl.BlockSpec((L, wq), lambda b, g, c: (rows(b, c), c_f // wq + g)),
                pl.BlockSpec((L, wv), lambda b, g, c: (rows(b, c), c_i // wv + g)),
                pl.BlockSpec((L, wv), lambda b, g, c: (rows(b, c), c_g // wv + g)),
                pl.BlockSpec((1, HP, DV, DK), lambda b, g, c: (b, g, 0, 0)),
                pl.BlockSpec((lb_logits.shape[0], wq), lambda b, g, c: (0, g)),
                pl.BlockSpec((1, wv), lambda b, g, c: (0, g))]
    args = [a_q, a_f, a_i, a_g, s0t, lb_logits, hn]
    kern, in_specs, args, aliases = _alias_prev(kern, in_specs, args, o_prev)
    return pl.pallas_call(
        kern,
        grid=(B, NH // HP, nC),
        in_specs=in_specs,
        out_specs=[pl.BlockSpec((L, wv), lambda b, g, c: (rows(b, c), g)),
                   pl.BlockSpec((1, HP, DV, DK), lambda b, g, c: (b, g, 0, 0))],
        out_shape=[jax.ShapeDtypeStruct((a_q.shape[0], NH * DV), BF16),
                   jax.ShapeDtypeStruct((B, NH, DV, DK), F32)],
        scratch_shapes=[pltpu.VMEM((HP, DV, DK), F32)],
        input_output_aliases=aliases,
        compiler_params=_cparams(("arbitrary", "arbitrary", "arbitrary")),
        name="mixer_hgrn2",
    )(*args)


def _new_expert(be_ref):
    b = pl.program_id(1)
    return (b == 0) | (be_ref[b] != be_ref[jnp.maximum(b - 1, 0)])


def _round_chunks(chunk_refs, w_s):
    rc = chunk_refs[0].shape[0]
    for c, ref in enumerate(chunk_refs):
        w_s[c * rc:(c + 1) * rc, :] = ref[...].astype(BF16)


def _gmm1_kernel(be_ref, na_ref, x_ref, *refs):
    w1_refs, w3_refs = refs[:W_CHUNKS], refs[W_CHUNKS:2 * W_CHUNKS]
    o_ref, w1_s, w3_s = refs[2 * W_CHUNKS:]

    @pl.when(_new_expert(be_ref))
    def _():
        _round_chunks(w1_refs, w1_s)
        _round_chunks(w3_refs, w3_s)

    @pl.when(pl.program_id(1) < na_ref[0])
    def _():
        x_lo, x_hi = _unpack_bf16_pairs(x_ref[...])
        half = x_lo.shape[1]
        up = _dot(x_lo, w1_s[0:half, :]) + _dot(x_hi, w1_s[half:, :])
        gate = _dot(x_lo, w3_s[0:half, :]) + _dot(x_hi, w3_s[half:, :])
        o_ref[...] = (_silu(up) * gate).astype(o_ref.dtype)

    @pl.when(pl.program_id(1) >= na_ref[0])
    def _():
        o_ref[...] = jnp.zeros_like(o_ref)


def _gmm2_kernel(be_ref, na_ref, h_ref, *refs):
    w2_refs = refs[:W_CHUNKS]
    rw_ref, o_ref, w2_s = refs[W_CHUNKS:]

    @pl.when(_new_expert(be_ref))
    def _():
        _round_chunks(w2_refs, w2_s)

    @pl.when(pl.program_id(1) < na_ref[0])
    def _():
        o_ref[...] = _dot(h_ref[...], w2_s[...]) * rw_ref[...]

    @pl.when(pl.program_id(1) >= na_ref[0])
    def _():
        o_ref[...] = jnp.zeros_like(o_ref)


def _expert_mlp(xg, roww, blk_e, n_act, w1, w3, w2, layer):
    rows, dh = xg.shape
    d, f = w1.shape[2], w1.shape[3]
    r = MOE_ROWS
    nb = rows // r
    tf, td = f // 2, d // 2

    def act(b, na):
        return jnp.minimum(b, na[0] - 1)

    def chunks(nrows, ncols):
        rc = nrows // W_CHUNKS
        return [pl.BlockSpec((None, None, rc, ncols),
                             lambda j, b, be, na, c=c: (layer, be[b], c, j))
                for c in range(W_CHUNKS)]

    hb = pl.pallas_call(
        _gmm1_kernel,
        grid_spec=pltpu.PrefetchScalarGridSpec(
            num_scalar_prefetch=2, grid=(f // tf, nb),
            in_specs=[pl.BlockSpec((r, dh), lambda j, b, be, na: (act(b, na), 0))]
            + chunks(d, tf) + chunks(d, tf),
            out_specs=pl.BlockSpec((r, tf), lambda j, b, be, na: (b, j)),
            scratch_shapes=[pltpu.VMEM((d, tf), BF16), pltpu.VMEM((d, tf), BF16)]),
        out_shape=jax.ShapeDtypeStruct((rows, f), BF16),
        compiler_params=_cparams(("arbitrary", "arbitrary")),
        name="expert_up",
    )(blk_e, n_act, xg, *([w1] * W_CHUNKS), *([w3] * W_CHUNKS))
    return pl.pallas_call(
        _gmm2_kernel,
        grid_spec=pltpu.PrefetchScalarGridSpec(
            num_scalar_prefetch=2, grid=(d // td, nb),
            in_specs=[pl.BlockSpec((r, f), lambda j, b, be, na: (act(b, na), 0))]
            + chunks(f, td)
            + [pl.BlockSpec((r, 1), lambda j, b, be, na: (act(b, na), 0))],
            out_specs=pl.BlockSpec((r, td), lambda j, b, be, na: (b, j)),
            scratch_shapes=[pltpu.VMEM((f, td), BF16)]),
        out_shape=jax.ShapeDtypeStruct((rows, d), F32),
        compiler_params=_cparams(("arbitrary", "arbitrary")),
        name="expert_down",
    )(blk_e, n_act, hb, *([w2] * W_CHUNKS), roww)


def _route(logits):
    n = logits.shape[0]
    pg = jax.nn.softmax(logits[:, :N_GROUPS], axis=-1)
    grp = jnp.argmax(pg, axis=-1)
    p_grp = jnp.max(pg, axis=-1)
    le = logits[:, N_GROUPS:N_GROUPS + N_EXPERTS].reshape(n, N_GROUPS, EXPERTS_PER_GROUP)
    le_g = le[jnp.arange(n), grp]
    top_logit, top_j = lax.top_k(le_g, TOP_K)
    wts = jax.nn.softmax(top_logit, axis=-1) * p_grp[:, None]
    eid = (grp[:, None] * EXPERTS_PER_GROUP + top_j).astype(jnp.int32)
    return eid, wts


def _dispatch(eid, wts):
    n = eid.shape[0]
    a = n * TOP_K
    r = MOE_ROWS
    nb = -(-a // r) + N_EXPERTS
    flat_e = eid.reshape(a)
    order = jnp.argsort(flat_e)
    se = flat_e[order]
    counts = jnp.bincount(flat_e, length=N_EXPERTS)
    padded = (counts + r - 1) // r * r
    pend = jnp.cumsum(padded)
    pstart = pend - padded
    start = jnp.cumsum(counts) - counts
    dest = (pstart[se] + jnp.arange(a, dtype=jnp.int32) - start[se]).astype(jnp.int32)
    rows = jnp.zeros((nb * r,), jnp.int32).at[dest].set((order // TOP_K).astype(jnp.int32))
    roww = jnp.zeros((nb * r,), F32).at[dest].set(wts.reshape(a)[order])
    n_act = (pend[-1] // r).astype(jnp.int32)
    blk = jnp.arange(nb, dtype=jnp.int32)
    blk_e = jnp.minimum(jnp.searchsorted(pend, jnp.minimum(blk, n_act - 1) * r, side='right'),
                        N_EXPERTS - 1).astype(jnp.int32)
    slot = jnp.zeros((a,), jnp.int32).at[order].set(dest).reshape(n, TOP_K)
    return rows, roww.reshape(nb * r, 1), blk_e, n_act.reshape(1), slot


def kernel(x_prompt, x_sample, c_prompt, c_sample, state_a_C, state_a_n, state_a_m, state_a_conv, state_b_S, state_c_S, w_ada, b_ada, norm_mix, w_in, conv_w, conv_b, b_gate_a, hn_a, w_gk2, b_gk, hn_b, lb_logits, hn_c, w_br_a, w_br_b, w_br_c, w_out, norm_moe, w_rg, b_rg, w_re, b_re, w_exp1, w_exp3, w_exp2, norm_final):
    depth = w_ada.shape[0]
    bp, tp, d = x_prompt.shape
    bs, ts, _ = x_sample.shape
    nh_a, dv_a, dk_a = state_a_C.shape[2:]
    nh_b, dk_b, dv_b = state_b_S.shape[2:]
    nh_c, dk_c, dv_c = state_c_S.shape[2:]
    gate_rank = w_gk2.shape[1]
    qk_a, w_a = nh_a * dk_a, nh_a * dv_a
    qk_b, w_b = nh_b * dk_b, nh_b * dv_b
    qk_c, w_c = nh_c * dk_c, nh_c * dv_c
    mp, ms = bp * tp, bs * ts
    m = mp + ms
    assert bp == 1 and 1 + bs <= MOD_ROWS and m % TM_BIG == 0 and ms <= TM_SMALL

    sizes = (2 * qk_a, w_a, 2 * nh_a, w_a, qk_b, qk_b, w_b, gate_rank, w_b,
             qk_c, qk_c, w_c, w_c, N_BRANCH * d)
    names = ("a_qk", "a_v", "a_if", "a_o", "b_q", "b_k", "b_v", "b_lr", "b_g",
             "c_q", "c_f", "c_i", "c_g", "gates")
    src, o = {}, 0
    for nm, sz in zip(names, sizes):
        src[nm] = o
        o += sz
    n_src = o
    lr_lane = src["b_lr"] % LANES
    assert src["a_if"] % LANES == 0 and lr_lane + gate_rank <= LANES
    ranges = ((src["a_qk"], src["a_if"], ("a_qk", "a_v")),
              (src["a_o"], src["b_lr"], ("a_o", "b_q", "b_k", "b_v")),
              (src["b_g"], n_src, ("b_g", "c_q", "c_f", "c_i", "c_g", "gates")))
    where = {}
    for ri, (s0, s1, members) in enumerate(ranges):
        for nm in members:
            where[nm] = (ri, src[nm] - s0)
    gate_windows = (src["a_if"], src["b_lr"] - lr_lane)

    x = jnp.concatenate([x_prompt.reshape(mp, d), x_sample.reshape(ms, d)], axis=0)
    row_cond = jnp.concatenate([jnp.zeros((mp,), jnp.int32),
                                1 + jnp.arange(ms, dtype=jnp.int32) // ts])
    rsel = (row_cond[:, None] == jnp.arange(MOD_ROWS, dtype=jnp.int32)[None, :]).astype(BF16)
    c_all = jnp.concatenate([c_prompt, c_sample,
                             jnp.zeros((MOD_ROWS - bp - bs, d), F32)], axis=0)
    mod = _modulation(c_all, w_ada, b_ada)

    zeros = lambda *s: jnp.zeros(s, F32)
    pad_m = lambda mm: jnp.pad(mm, ((0, 0), (0, LANES - mm.shape[1])))[:, None, :]
    outs_p = [[] for _ in range(6)]
    outs_s = [[] for _ in range(6)]

    w_br_a16, w_br_b16, w_br_c16 = w_br_a.astype(BF16), w_br_b.astype(BF16), w_br_c.astype(BF16)
    w_out16 = w_out.astype(BF16)

    for l in range(depth):
        mod_l = mod[l:l + 1]
        h = _norm_mod(x, rsel, norm_mix[l], mod_l, 0, 1)
        ps = [_matmul(h, _pack_cols(w_in, l, s0, s1 - s0), F32, TM_BIG, TN_PROJ)
              for s0, s1, _ in ranges]
        w_gw = jnp.concatenate([w_in[l, :, s:s + LANES] for s in gate_windows], axis=1)
        p_gw = _matmul(h, w_gw.astype(BF16), F32, TM_BIG, 2 * LANES)

        def at(nm, extra=0):
            ri, c = where[nm]
            return ps[ri], c + extra

        bias_if = jnp.pad(b_gate_a[l].reshape(1, 2 * nh_a), ((0, 0), (0, LANES - 2 * nh_a)))
        w_gk2p = jnp.pad(w_gk2[l], ((lr_lane, LANES - lr_lane - gate_rank), (0, 0)))
        groups = (
            dict(row0=0, B=bp, T=tp, La=64, Lg=64,
                 conv0=zeros(bp, CONV_W - 1, 2 * qk_a), c0=zeros(bp, nh_a, dv_a, dk_a),
                 n0=zeros(bp, nh_a, dk_a), m0=zeros(bp, 1, LANES),
                 sb0=zeros(bp, nh_b, dv_b, dk_b), sc0=zeros(bp, nh_c, dv_c, dk_c)),
            dict(row0=mp, B=bs, T=ts, La=ts, Lg=ts,
                 conv0=state_a_conv[l], c0=state_a_C[l], n0=state_a_n[l],
                 m0=pad_m(state_a_m[l]),
                 sb0=jnp.swapaxes(state_b_S[l], -1, -2), sc0=jnp.swapaxes(state_c_S[l], -1, -2)),
        )
        o_a, o_b, o_c = (jnp.zeros((m, w), BF16) for w in (w_a, w_b, w_c))
        for g, outs in zip(groups, (outs_p, outs_s)):
            o_a, a_c, a_n, a_m, a_conv = _mlstm(
                (at("a_qk"), at("a_qk", qk_a), at("a_v"), at("a_o"), (p_gw, 0)),
                g["conv0"], g["c0"], g["n0"], g["m0"], conv_w[l], conv_b[l].reshape(1, -1),
                bias_if, hn_a[l].reshape(1, -1), o_a,
                row0=g["row0"], B=g["B"], T=g["T"], L=g["La"], NH=nh_a, DK=dk_a, DV=dv_a)
            o_b, b_st = _gla_b(
                (at("b_q"), at("b_k"), at("b_v"), at("b_g"), (p_gw, LANES)),
                g["sb0"], w_gk2p, b_gk[l].reshape(1, -1), hn_b[l].reshape(1, -1), o_b,
                row0=g["row0"], B=g["B"], T=g["T"], L=g["Lg"], NH=nh_b, DK=dk_b, DV=dv_b, HP=4)
            o_c, c_st = _gla_c(
                (at("c_q"), at("c_f"), at("c_i"), at("c_g")),
                g["sc0"], lb_logits, hn_c[l].reshape(1, -1), o_c,
                layer=l, row0=g["row0"], B=g["B"], T=g["T"], L=g["Lg"],
                NH=nh_c, DK=dk_c, DV=dv_c, HP=8)
            for lst, val in zip(outs, (a_c, a_n, a_m[:, 0, :nh_a], a_conv,
                                       jnp.swapaxes(b_st, -1, -2), jnp.swapaxes(c_st, -1, -2))):
                lst.append(val)

        merged = _merge(o_a, o_b, o_c, w_br_a16, w_br_b16, w_br_c16, l, *at("gates"))
        x = _out_proj(merged, w_out16, l, x, rsel, mod_l, 2)

        w_r = jnp.pad(jnp.concatenate([w_rg[l], w_re[l]], axis=1),
                      ((0, 0), (0, LANES - N_GROUPS - N_EXPERTS)))
        b_r = jnp.pad(jnp.concatenate([b_rg[l], b_re[l]]),
                      (0, LANES - N_GROUPS - N_EXPERTS)).reshape(1, LANES)
        h2, logits = _norm_mod_router(x, rsel, norm_moe[l], mod_l, 3, 4, w_r, b_r)
        eid, wts = _route(logits)
        rows, roww, blk_e, n_act, slot = _dispatch(eid, wts)
        yb = _expert_mlp(h2[rows], roww, blk_e, n_act, w_exp1, w_exp3, w_exp2, l)
        x = _gated_add(x, yb[slot[:, 0]], yb[slot[:, 1]], rsel, mod_l, 5)

    y_prompt = _final_norm(x, norm_final, 0, mp, 512).reshape(bp, tp, d)
    y_sample = _final_norm(x, norm_final, mp, ms, ms).reshape(bs, ts, d)
    stack = lambda lst: jnp.stack(lst)
    return (y_prompt, y_sample,
            stack(outs_p[0]), stack(outs_p[1]), stack(outs_p[2]), stack(outs_p[3]),
            stack(outs_p[4]), stack(outs_p[5]),
            stack(outs_s[0]), stack(outs_s[1]), stack(outs_s[2]), stack(outs_s[3]),
            stack(outs_s[4]), stack(outs_s[5]))
```

```python
import functools

import jax
import jax.numpy as jnp
import numpy as np
from jax import lax
from jax.experimental import pallas as pl
from jax.experimental.pallas import tpu as pltpu

F32 = jnp.float32
BF16 = jnp.bfloat16

NORM_EPS = 1e-6
NEG_BIG = -1e30
F_TINY = 1e-30
GLA_NORM = 16.0
CONV_W = 4
TOP_K = 2
N_GROUPS = 4
EXPERTS_PER_GROUP = 8
N_EXPERTS = N_GROUPS * EXPERTS_PER_GROUP
N_BRANCH = 3

LANES = 128
MOD_ROWS = 16
VMEM_LIMIT = 56 * 1024 * 1024

TM_BIG = 1040
TM_SMALL = 520
MOE_ROWS = 256
W_CHUNKS = 4
TN_PROJ = 1024
SUB = 16
SAFE_LOG = 60.0


def _cparams(sem):
    return pltpu.CompilerParams(dimension_semantics=sem, vmem_limit_bytes=VMEM_LIMIT)


def _dot(a, b):
    return jnp.dot(a, b, preferred_element_type=F32)


def _dot_nt(a, b):
    return lax.dot_general(a, b, (((1,), (1,)), ((), ())), preferred_element_type=F32)


def _dot_tn(a, b):
    return lax.dot_general(a, b, (((0,), (0,)), ((), ())), preferred_element_type=F32)


def _split3(x):
    hi = x.astype(BF16)
    r = x - hi.astype(F32)
    mid = r.astype(BF16)
    lo = (r - mid.astype(F32)).astype(BF16)
    return hi, mid, lo


def _dot_sel(sel, x):
    hi, mid, lo = _split3(x)
    return _dot(sel, hi) + _dot(sel, mid) + _dot(sel, lo)


def _dot_nt_sel(sel, x):
    hi, mid, lo = _split3(x)
    return _dot_nt(sel, hi) + _dot_nt(sel, mid) + _dot_nt(sel, lo)


def _dot_nt_f32(a, b):
    a_hi = a.astype(BF16)
    a_lo = (a - a_hi.astype(F32)).astype(BF16)
    b_hi = b.astype(BF16)
    b_lo = (b - b_hi.astype(F32)).astype(BF16)
    return _dot_nt(a_hi, b_hi) + _dot_nt(a_hi, b_lo) + _dot_nt(a_lo, b_hi)


def _r16(x):
    return x.astype(BF16).astype(F32)


def _pack_bf16_pairs(xb):
    n = xb.shape[1] // 2
    bits = pltpu.bitcast(xb.astype(F32), jnp.int32)
    return lax.shift_right_logical(bits[:, :n], 16) | bits[:, n:]


def _unpack_bf16_pairs(w):
    lo = pltpu.bitcast(lax.shift_left(w, 16), F32).astype(BF16)
    hi = pltpu.bitcast(w & jnp.int32(-65536), F32).astype(BF16)
    return lo, hi


def _sigmoid(x):
    return jax.nn.sigmoid(x)


def _silu(x):
    return x * jax.nn.sigmoid(x)


def _log_sigmoid(x):
    return jnp.minimum(x, 0.0) - jnp.log1p(jnp.exp(-jnp.abs(x)))


def _rms(x):
    return x * lax.rsqrt(jnp.mean(x * x, axis=-1, keepdims=True) + NORM_EPS)


def _tril(n):
    r = lax.broadcasted_iota(jnp.int32, (n, n), 0)
    c = lax.broadcasted_iota(jnp.int32, (n, n), 1)
    return jnp.where(r >= c, 1.0, 0.0).astype(BF16)


def _eye(n):
    r = lax.broadcasted_iota(jnp.int32, (n, n), 0)
    c = lax.broadcasted_iota(jnp.int32, (n, n), 1)
    return jnp.where(r == c, 1.0, 0.0).astype(BF16)


def _mod_kernel(c_ref, w_ref, b_ref, o_ref):
    c = c_ref[...]
    o_ref[0] = _dot(_silu(c).astype(BF16), w_ref[0].astype(BF16)) + b_ref[0]


def _modulation(c_all, w_ada, b_ada):
    depth, d, n = w_ada.shape
    tn = 1024
    return pl.pallas_call(
        _mod_kernel,
        grid=(depth, n // tn),
        in_specs=[pl.BlockSpec((MOD_ROWS, d), lambda l, j: (0, 0)),
                  pl.BlockSpec((1, d, tn), lambda l, j: (l, 0, j)),
                  pl.BlockSpec((1, 1, tn), lambda l, j: (l, 0, j))],
        out_specs=pl.BlockSpec((1, MOD_ROWS, tn), lambda l, j: (l, 0, j)),
        out_shape=jax.ShapeDtypeStruct((depth, MOD_ROWS, n), F32),
        compiler_params=_cparams(("arbitrary", "arbitrary")),
        name="modulation",
    )(c_all, w_ada, b_ada.reshape(depth, 1, n))


def _row_mod(mixed, r_ref, m_ref):
    if mixed:
        return _dot_sel(r_ref[...], m_ref[0])
    return m_ref[0, 0:1, :]


def _norm_mod_kernel(x_ref, r_ref, g_ref, sh_ref, sc_ref, o_ref):
    last = pl.num_programs(0) - 1

    def body(mixed):
        y = _rms(x_ref[...]) * g_ref[...]
        o_ref[...] = (y * (1.0 + _row_mod(mixed, r_ref, sc_ref)) + _row_mod(mixed, r_ref, sh_ref)
                      ).astype(o_ref.dtype)

    pl.when(pl.program_id(0) != last)(lambda: body(False))
    pl.when(pl.program_id(0) == last)(lambda: body(True))


def _norm_mod(x, rsel, gain, mod_l, sh_idx, sc_idx):
    m, d = x.shape
    tm = TM_SMALL
    return pl.pallas_call(
        _norm_mod_kernel,
        grid=(m // tm,),
        in_specs=[pl.BlockSpec((tm, d), lambda i: (i, 0)),
                  pl.BlockSpec((tm, MOD_ROWS), lambda i: (i, 0)),
                  pl.BlockSpec((1, d), lambda i: (0, 0)),
                  pl.BlockSpec((1, MOD_ROWS, d), lambda i: (0, 0, sh_idx)),
                  pl.BlockSpec((1, MOD_ROWS, d), lambda i: (0, 0, sc_idx))],
        out_specs=pl.BlockSpec((tm, d), lambda i: (i, 0)),
        out_shape=jax.ShapeDtypeStruct((m, d), BF16),
        compiler_params=_cparams(("arbitrary",)),
        name="norm_mod",
    )(x, rsel, gain.reshape(1, d), mod_l, mod_l)


def _norm_mod_router_kernel(x_ref, r_ref, g_ref, sh_ref, sc_ref, wr_ref, br_ref, o_ref, lg_ref):
    last = pl.num_programs(0) - 1

    def body(mixed):
        y = _rms(x_ref[...]) * g_ref[...]
        h = y * (1.0 + _row_mod(mixed, r_ref, sc_ref)) + _row_mod(mixed, r_ref, sh_ref)
        hb = h.astype(BF16)
        o_ref[...] = _pack_bf16_pairs(hb)
        lg_ref[...] = _dot(hb, wr_ref[...].astype(BF16)) + br_ref[...]

    pl.when(pl.program_id(0) != last)(lambda: body(False))
    pl.when(pl.program_id(0) == last)(lambda: body(True))


def _norm_mod_router(x, rsel, gain, mod_l, sh_idx, sc_idx, w_r, b_r):
    m, d = x.shape
    tm = TM_SMALL
    return pl.pallas_call(
        _norm_mod_router_kernel,
        grid=(m // tm,),
        in_specs=[pl.BlockSpec((tm, d), lambda i: (i, 0)),
                  pl.BlockSpec((tm, MOD_ROWS), lambda i: (i, 0)),
                  pl.BlockSpec((1, d), lambda i: (0, 0)),
                  pl.BlockSpec((1, MOD_ROWS, d), lambda i: (0, 0, sh_idx)),
                  pl.BlockSpec((1, MOD_ROWS, d), lambda i: (0, 0, sc_idx)),
                  pl.BlockSpec((d, LANES), lambda i: (0, 0)),
                  pl.BlockSpec((1, LANES), lambda i: (0, 0))],
        out_specs=[pl.BlockSpec((tm, d // 2), lambda i: (i, 0)),
                   pl.BlockSpec((tm, LANES), lambda i: (i, 0))],
        out_shape=[jax.ShapeDtypeStruct((m, d // 2), jnp.int32),
                   jax.ShapeDtypeStruct((m, LANES), F32)],
        compiler_params=_cparams(("arbitrary",)),
        name="norm_mod_router",
    )(x, rsel, gain.reshape(1, d), mod_l, mod_l, w_r, b_r)


def _final_norm_kernel(x_ref, g_ref, o_ref):
    o_ref[...] = _rms(x_ref[...]) * g_ref[...]


def _final_norm(x, gain, row0, rows, tm):
    d = x.shape[1]
    blk0 = row0 // tm
    return pl.pallas_call(
        _final_norm_kernel,
        grid=(rows // tm,),
        in_specs=[pl.BlockSpec((tm, d), lambda i: (blk0 + i, 0)),
                  pl.BlockSpec((1, d), lambda i: (0, 0))],
        out_specs=pl.BlockSpec((tm, d), lambda i: (i, 0)),
        out_shape=jax.ShapeDtypeStruct((rows, d), F32),
        compiler_params=_cparams(("arbitrary",)),
        name="final_norm",
    )(x, gain.reshape(1, d))


def _matmul_kernel(a_ref, b_ref, o_ref):
    o_ref[...] = _dot(a_ref[...], b_ref[...]).astype(o_ref.dtype)


def _matmul(a, b, out_dtype, tm, tn):
    m, k = a.shape
    n = b.shape[1]
    return pl.pallas_call(
        _matmul_kernel,
        grid=(m // tm, n // tn),
        in_specs=[pl.BlockSpec((tm, k), lambda i, j: (i, 0)),
                  pl.BlockSpec((k, tn), lambda i, j: (0, j))],
        out_specs=pl.BlockSpec((tm, tn), lambda i, j: (i, j)),
        out_shape=jax.ShapeDtypeStruct((m, n), out_dtype),
        compiler_params=_cparams(("arbitrary", "arbitrary")),
        name="in_proj",
    )(a, b)


def _merge_kernel(oa_ref, ob_ref, oc_ref, wa_ref, wb_ref, wc_ref, ga_ref, gb_ref, gc_ref, o_ref):
    acc = _sigmoid(ga_ref[...]) * _dot(oa_ref[...], wa_ref[...])
    acc = acc + _sigmoid(gb_ref[...]) * _dot(ob_ref[...], wb_ref[...])
    acc = acc + _sigmoid(gc_ref[...]) * _dot(oc_ref[...], wc_ref[...])
    o_ref[...] = acc.astype(o_ref.dtype)


def _merge(o_a, o_b, o_c, w_a, w_b, w_c, layer, p, gate_col0):
    m, kw = o_a.shape
    d = w_a.shape[2]
    tm, tn = TM_SMALL, 512
    g0 = gate_col0 // tn
    gstride = d // tn
    o_spec = pl.BlockSpec((tm, kw), lambda j, i: (i, 0))
    w_spec = pl.BlockSpec((None, kw, tn), lambda j, i: (layer, 0, j))

    def g_spec(br):
        return pl.BlockSpec((tm, tn), lambda j, i: (i, g0 + br * gstride + j))

    return pl.pallas_call(
        _merge_kernel,
        grid=(d // tn, m // tm),
        in_specs=[o_spec, o_spec, o_spec, w_spec, w_spec, w_spec, g_spec(0), g_spec(1), g_spec(2)],
        out_specs=pl.BlockSpec((tm, tn), lambda j, i: (i, j)),
        out_shape=jax.ShapeDtypeStruct((m, d), BF16),
        compiler_params=_cparams(("arbitrary", "arbitrary")),
        name="merge",
    )(o_a, o_b, o_c, w_a, w_b, w_c, p, p, p)


def _out_proj_kernel(a_ref, w_ref, x_ref, r_ref, g_ref, o_ref):
    last = pl.num_programs(0) - 1
    acc = _dot(a_ref[...], w_ref[...])

    def body(mixed):
        o_ref[...] = x_ref[...] + _row_mod(mixed, r_ref, g_ref) * acc

    pl.when(pl.program_id(0) != last)(lambda: body(False))
    pl.when(pl.program_id(0) == last)(lambda: body(True))


def _out_proj(a, w, layer, x, rsel, mod_l, g_idx):
    m, k = a.shape
    d = w.shape[2]
    tm, tn = TM_BIG, 512
    nj = d // tn
    return pl.pallas_call(
        _out_proj_kernel,
        grid=(m // tm, nj),
        in_specs=[pl.BlockSpec((tm, k), lambda i, j: (i, 0)),
                  pl.BlockSpec((None, k, tn), lambda i, j: (layer, 0, j)),
                  pl.BlockSpec((tm, tn), lambda i, j: (i, j)),
                  pl.BlockSpec((tm, MOD_ROWS), lambda i, j: (i, 0)),
                  pl.BlockSpec((1, MOD_ROWS, tn), lambda i, j: (0, 0, g_idx * nj + j))],
        out_specs=pl.BlockSpec((tm, tn), lambda i, j: (i, j)),
        out_shape=jax.ShapeDtypeStruct((m, d), F32),
        compiler_params=_cparams(("arbitrary", "arbitrary")),
        name="out_proj",
    )(a, w, x, rsel, mod_l)


def _gated_add_kernel(x_ref, y0_ref, y1_ref, r_ref, g_ref, o_ref):
    last = pl.num_programs(0) - 1

    def body(mixed):
        o_ref[...] = x_ref[...] + _row_mod(mixed, r_ref, g_ref) * (y0_ref[...] + y1_ref[...])

    pl.when(pl.program_id(0) != last)(lambda: body(False))
    pl.when(pl.program_id(0) == last)(lambda: body(True))


def _gated_add(x, y0, y1, rsel, mod_l, g_idx):
    m, d = x.shape
    tm, tn = TM_SMALL, 1024
    nj = d // tn
    spec = pl.BlockSpec((tm, tn), lambda i, j: (i, j))
    return pl.pallas_call(
        _gated_add_kernel,
        grid=(m // tm, nj),
        in_specs=[spec, spec, spec,
                  pl.BlockSpec((tm, MOD_ROWS), lambda i, j: (i, 0)),
                  pl.BlockSpec((1, MOD_ROWS, tn), lambda i, j: (0, 0, g_idx * nj + j))],
        out_specs=spec,
        out_shape=jax.ShapeDtypeStruct((m, d), F32),
        compiler_params=_cparams(("arbitrary", "arbitrary")),
        name="moe_combine",
    )(x, y0, y1, rsel, mod_l)


def _alias_prev(kern, in_specs, args, o_prev):
    if o_prev is None:
        return kern, in_specs, args, {}
    n = len(args)

    def wrapped(*refs):
        return kern(*refs[:n], *refs[n + 1:])

    return wrapped, in_specs + [pl.BlockSpec(memory_space=pl.ANY)], args + [o_prev], {n: 0}


def _mlstm_kernel(qp_ref, kp_ref, v_ref, og_ref, if_ref, conv0_ref, c0_ref, n0_ref, m0_ref,
                  cw_ref, cb_ref, bif_ref, hn_ref,
                  o_ref, cout_ref, nout_ref, mout_ref, convout_ref,
                  c_s, n_s, m_s, ubuf, q_s, k_s, *, L, NH, DK, DV):
    ci = pl.program_id(1)
    QK = NH * DK

    @pl.when(ci == 0)
    def _():
        c_s[...] = c0_ref[0]
        n_s[...] = n0_ref[0]
        m_s[...] = m0_ref[0]
        ubuf[8 - (CONV_W - 1):8, :] = conv0_ref[0]

    ubuf[8:8 + L, 0:QK] = qp_ref[...]
    ubuf[8:8 + L, QK:2 * QK] = kp_ref[...]
    y = cb_ref[...]
    for j in range(CONV_W):
        y = y + ubuf[8 - (CONV_W - 1) + j:8 - (CONV_W - 1) + j + L, :] * cw_ref[j:j + 1, :]
    ubuf[0:8, :] = ubuf[L:L + 8, :]
    qk = _silu(y)
    q_s[...] = qk[:, 0:QK] * (DK ** -0.5)
    k_s[...] = qk[:, QK:2 * QK]

    ifv = if_ref[...] + bif_ref[...]
    b_all = _dot_sel(_tril(L), _log_sigmoid(ifv))
    eye = _eye(LANES)
    ig_t = _dot_nt_sel(eye, ifv)
    b_t = _dot_nt_sel(eye, b_all)
    row = lax.broadcasted_iota(jnp.int32, (L, L), 0)
    col = lax.broadcasted_iota(jnp.int32, (L, L), 1)
    causal = row >= col
    lane = lax.broadcasted_iota(jnp.int32, (1, LANES), 1)
    m_old = m_s[...]
    m_new = m_old

    for h in range(NH):
        b_c = b_all[:, NH + h:NH + h + 1]
        ig_c = ifv[:, h:h + 1]
        b_r = b_t[NH + h:NH + h + 1, :]
        ig_r = ig_t[h:h + 1, :]
        m_prev = m_old[:, h:h + 1]
        log_d = jnp.where(causal, b_c - b_r + ig_r, NEG_BIG)
        inter = b_c + m_prev
        mt = jnp.maximum(jnp.max(log_d, axis=-1, keepdims=True), inter)
        dm = jnp.exp(log_d - mt)
        sc = jnp.exp(inter - mt)
        qh = q_s[:, h * DK:(h + 1) * DK]
        kh = k_s[:, h * DK:(h + 1) * DK]
        vh = v_ref[:, h * DV:(h + 1) * DV]
        qb, kb, vb = qh.astype(BF16), kh.astype(BF16), vh.astype(BF16)
        c_h = c_s[h]
        n_h = n_s[h:h + 1, :]
        s = _dot_nt(qb, kb) * dm
        num = sc * _dot_nt(qb, c_h.astype(BF16)) + _dot(s.astype(BF16), vb)
        den = (sc * jnp.sum(qb.astype(F32) * _r16(n_h), axis=-1, keepdims=True)
               + jnp.sum(s, axis=-1, keepdims=True))
        hc = num / jnp.maximum(jnp.abs(den), jnp.exp(-mt))
        mt_l = mt[L - 1:L, :]
        w_l = jnp.exp((b_c[L - 1:L, :] - b_c) + ig_c - mt_l)
        s_l = sc[L - 1:L, :]
        c_s[h] = s_l * c_h + _dot_tn(vb, (kh * w_l).astype(BF16))
        n_s[h:h + 1, :] = s_l * n_h + jnp.sum(_r16(w_l) * kb.astype(F32), axis=0, keepdims=True)
        m_new = jnp.where(lane == h, mt_l, m_new)
        out = _rms(hc) * hn_ref[:, h * DV:(h + 1) * DV] * _sigmoid(og_ref[:, h * DV:(h + 1) * DV])
        o_ref[:, h * DV:(h + 1) * DV] = out.astype(o_ref.dtype)

    m_s[...] = m_new

    @pl.when(ci == pl.num_programs(1) - 1)
    def _():
        cout_ref[0] = c_s[...]
        nout_ref[0] = n_s[...]
        mout_ref[0] = m_s[...]
        convout_ref[0] = ubuf[8 - (CONV_W - 1):8, :]


def _mlstm(srcs, conv0, c0, n0, m0, conv_w, conv_b, bias_if, hn, o_prev,
           *, row0, B, T, L, NH, DK, DV):
    QK, W = NH * DK, NH * DV
    nC = T // L
    rb0 = row0 // L
    (a_q, c_q), (a_k, c_k), (a_v, c_v), (a_o, c_o), (a_if, c_if) = srcs

    def rows(b, c):
        return rb0 + b * nC + c

    kern = functools.partial(_mlstm_kernel, L=L, NH=NH, DK=DK, DV=DV)
    in_specs = [pl.BlockSpec((L, QK), lambda b, c: (rows(b, c), c_q // QK)),
                  pl.BlockSpec((L, QK), lambda b, c: (rows(b, c), c_k // QK)),
                  pl.BlockSpec((L, W), lambda b, c: (rows(b, c), c_v // W)),
                  pl.BlockSpec((L, W), lambda b, c: (rows(b, c), c_o // W)),
                  pl.BlockSpec((L, LANES), lambda b, c: (rows(b, c), c_if // LANES)),
                  pl.BlockSpec((1, CONV_W - 1, 2 * QK), lambda b, c: (b, 0, 0)),
                  pl.BlockSpec((1, NH, DV, DK), lambda b, c: (b, 0, 0, 0)),
                  pl.BlockSpec((1, NH, DK), lambda b, c: (b, 0, 0)),
                  pl.BlockSpec((1, 1, LANES), lambda b, c: (b, 0, 0)),
                  pl.BlockSpec((CONV_W, 2 * QK), lambda b, c: (0, 0)),
                  pl.BlockSpec((1, 2 * QK), lambda b, c: (0, 0)),
                  pl.BlockSpec((1, LANES), lambda b, c: (0, 0)),
                  pl.BlockSpec((1, W), lambda b, c: (0, 0))]
    args = [a_q, a_k, a_v, a_o, a_if, conv0, c0, n0, m0, conv_w, conv_b, bias_if, hn]
    kern, in_specs, args, aliases = _alias_prev(kern, in_specs, args, o_prev)
    return pl.pallas_call(
        kern,
        grid=(B, nC),
        in_specs=in_specs,
        out_specs=[pl.BlockSpec((L, W), lambda b, c: (rows(b, c), 0)),
                   pl.BlockSpec((1, NH, DV, DK), lambda b, c: (b, 0, 0, 0)),
                   pl.BlockSpec((1, NH, DK), lambda b, c: (b, 0, 0)),
                   pl.BlockSpec((1, 1, LANES), lambda b, c: (b, 0, 0)),
                   pl.BlockSpec((1, CONV_W - 1, 2 * QK), lambda b, c: (b, 0, 0))],
        out_shape=[jax.ShapeDtypeStruct((a_q.shape[0], W), BF16),
                   jax.ShapeDtypeStruct((B, NH, DV, DK), F32),
                   jax.ShapeDtypeStruct((B, NH, DK), F32),
                   jax.ShapeDtypeStruct((B, 1, LANES), F32),
                   jax.ShapeDtypeStruct((B, CONV_W - 1, 2 * QK), F32)],
        scratch_shapes=[pltpu.VMEM((NH, DV, DK), F32),
                        pltpu.VMEM((NH, DK), F32),
                        pltpu.VMEM((1, LANES), F32),
                        pltpu.VMEM((L + 8, 2 * QK), F32),
                        pltpu.VMEM((L, QK), F32),
                        pltpu.VMEM((L, QK), F32)],
        input_output_aliases=aliases,
        compiler_params=_cparams(("arbitrary", "arbitrary")),
        name="mixer_mlstm",
    )(*args)


def _gla_intra_exact(q, k, b, L, c):
    dk = q.shape[1]
    nsub = L // c
    q3 = q.reshape(nsub, c, dk)
    k3 = k.reshape(nsub, c, dk)
    b3 = b.reshape(nsub, c, dk)
    t_idx = lax.broadcasted_iota(jnp.int32, (1, c, 1), 1)
    s_idx = lax.broadcasted_iota(jnp.int32, (1, 1, c), 2)
    a_diag = jnp.zeros((nsub, c, c), F32)
    for s in range(c):
        arg = jnp.where(t_idx >= s, b3 - b3[:, s:s + 1, :], NEG_BIG)
        col_s = jnp.sum(q3 * k3[:, s:s + 1, :] * jnp.exp(arg), axis=-1, keepdims=True)
        a_diag = jnp.where(s_idx == s, col_s, a_diag)
    a_diag = a_diag.reshape(L, c)
    if nsub == 1:
        return a_diag
    row = lax.broadcasted_iota(jnp.int32, (L, L), 0)
    col = lax.broadcasted_iota(jnp.int32, (L, L), 1)
    rep_r = lax.broadcasted_iota(jnp.int32, (c, L), 0)
    rep_c = lax.broadcasted_iota(jnp.int32, (c, L), 1)
    rep = jnp.where((rep_c & (c - 1)) == rep_r, 1.0, 0.0).astype(BF16)
    a = jnp.where((row & -c) == (col & -c), _dot(a_diag.astype(BF16), rep), 0.0)
    blocks = [jnp.zeros((c, L), F32)]
    for i in range(1, nsub):
        r_i = b[i * c - 1:i * c, :]
        q_i = q[i * c:(i + 1) * c, :] * jnp.exp(b[i * c:(i + 1) * c, :] - r_i)
        k_i = k * jnp.exp(jnp.minimum(r_i - b, 0.0))
        blocks.append(_dot_nt_f32(q_i, k_i))
    return a + jnp.where((col & -c) < (row & -c), jnp.concatenate(blocks, axis=0), 0.0)


def _rows_of(vecs, c):
    n = vecs[0].shape[1]
    parts = [jnp.broadcast_to(v, (c, n)) for v in vecs]
    return parts[0] if len(parts) == 1 else jnp.concatenate(parts, axis=0)


def _gla_chunk_fact(q, k, v, b, st, L):
    c = min(SUB, L)
    nsub = L // c
    dk = q.shape[1]
    r = ([jnp.zeros((1, dk), F32)] + [b[j * c - 1:j * c, :] for j in range(1, nsub)]
         + [b[L - 1:L, :]])
    q_t = q * jnp.exp(b - _rows_of(r[:nsub], c))
    k_h = k * jnp.exp(_rows_of(r[1:], c) - b)
    o = _dot_nt((q_t * _rows_of([jnp.exp(rj) for rj in r[:nsub]], c)).astype(BF16),
                st.astype(BF16))
    row = lax.broadcasted_iota(jnp.int32, (L, L), 0)
    col = lax.broadcasted_iota(jnp.int32, (L, L), 1)
    blocks = []
    for i in range(nsub):
        scale = _rows_of([jnp.exp(r[i] - r[j + 1]) if j <= i else jnp.ones((1, dk), F32)
                          for j in range(nsub)], c)
        blocks.append(_dot_nt_f32(q_t[i * c:(i + 1) * c, :], k_h * scale))
    a = blocks[0] if nsub == 1 else jnp.concatenate(blocks, axis=0)
    a = jnp.where(col <= row, a, 0.0)
    vb = v.astype(BF16)
    o = o + _dot(a.astype(BF16), vb)
    k_l = k_h * _rows_of([jnp.exp(r[nsub] - rj) for rj in r[1:]], c)
    st_new = st * jnp.exp(r[nsub]) + _dot_tn(vb, k_l.astype(BF16))
    return o, st_new


def _gla_cum_decay(g, L):
    c = min(SUB, L)
    nsub = L // c
    b = _dot_sel(_tril(L), g)
    drops = [b[c - 1:c, :]] + [b[(i + 1) * c - 1:(i + 1) * c, :] - b[i * c - 1:i * c, :]
                               for i in range(1, nsub)]
    return b, jnp.min(drops[0] if nsub == 1 else jnp.concatenate(drops, axis=0))


def _either(pred, body):
    pl.when(pred)(lambda: body(True))
    pl.when(jnp.logical_not(pred))(lambda: body(False))


def _gla_chunk(q, k, v, b, st, L, fact):
    if fact:
        return _gla_chunk_fact(q, k, v, b, st, L)
    c = min(SUB, L)
    b_l = b[L - 1:L, :]
    o = _dot_nt((q * jnp.exp(b)).astype(BF16), st.astype(BF16))
    a = _gla_intra_exact(q, k, b, L, c)
    vb = v.astype(BF16)
    o = o + _dot(a.astype(BF16), vb)
    st_new = st * jnp.exp(b_l) + _dot_tn(vb, (k * jnp.exp(b_l - b)).astype(BF16))
    return o, st_new


def _gla_b_kernel(q_ref, k_ref, v_ref, gt_ref, lr_ref, s0_ref, w2_ref, bgk_ref, hn_ref,
                  o_ref, sout_ref, st_s, *, L, HP, DK, DV):
    ci = pl.program_id(2)

    @pl.when(ci == 0)
    def _():
        st_s[...] = s0_ref[0]

    z = _dot(lr_ref[...].astype(BF16), w2_ref[...].astype(BF16)) + bgk_ref[...]
    b, worst = _gla_cum_decay(_log_sigmoid(z) / GLA_NORM, L)

    def body(fact):
        for h in range(HP):
            sl = slice(h * DK, (h + 1) * DK)
            sv = slice(h * DV, (h + 1) * DV)
            o, st_new = _gla_chunk(q_ref[:, sl] * (DK ** -0.5), k_ref[:, sl], v_ref[:, sv],
                                   b[:, sl], st_s[h], L, fact)
            st_s[h] = st_new
            o_ref[:, sv] = (_rms(o) * hn_ref[:, sv] * _silu(gt_ref[:, sv])).astype(o_ref.dtype)

    _either(worst > -SAFE_LOG, body)

    @pl.when(ci == pl.num_programs(2) - 1)
    def _():
        sout_ref[0] = st_s[...]


def _gla_b(srcs, s0t, w_gk2p, b_gk, hn, o_prev, *, row0, B, T, L, NH, DK, DV, HP):
    nC = T // L
    rb0 = row0 // L
    (a_q, c_q), (a_k, c_k), (a_v, c_v), (a_g, c_g), (a_lr, c_lr) = srcs
    wq, wv = HP * DK, HP * DV

    def rows(b, c):
        return rb0 + b * nC + c

    kern = functools.partial(_gla_b_kernel, L=L, HP=HP, DK=DK, DV=DV)
    in_specs = [pl.BlockSpec((L, wq), lambda b, g, c: (rows(b, c), c_q // wq + g)),
                pl.BlockSpec((L, wq), lambda b, g, c: (rows(b, c), c_k // wq + g)),
                pl.BlockSpec((L, wv), lambda b, g, c: (rows(b, c), c_v // wv + g)),
                pl.BlockSpec((L, wv), lambda b, g, c: (rows(b, c), c_g // wv + g)),
                pl.BlockSpec((L, LANES), lambda b, g, c: (rows(b, c), c_lr // LANES)),
                pl.BlockSpec((1, HP, DV, DK), lambda b, g, c: (b, g, 0, 0)),
                pl.BlockSpec((LANES, wq), lambda b, g, c: (0, g)),
                pl.BlockSpec((1, wq), lambda b, g, c: (0, g)),
                pl.BlockSpec((1, wv), lambda b, g, c: (0, g))]
    args = [a_q, a_k, a_v, a_g, a_lr, s0t, w_gk2p, b_gk, hn]
    kern, in_specs, args, aliases = _alias_prev(kern, in_specs, args, o_prev)
    return pl.pallas_call(
        kern,
        grid=(B, NH // HP, nC),
        in_specs=in_specs,
        out_specs=[pl.BlockSpec((L, wv), lambda b, g, c: (rows(b, c), g)),
                   pl.BlockSpec((1, HP, DV, DK), lambda b, g, c: (b, g, 0, 0))],
        out_shape=[jax.ShapeDtypeStruct((a_q.shape[0], NH * DV), BF16),
                   jax.ShapeDtypeStruct((B, NH, DV, DK), F32)],
        scratch_shapes=[pltpu.VMEM((HP, DV, DK), F32)],
        input_output_aliases=aliases,
        compiler_params=_cparams(("arbitrary", "arbitrary", "arbitrary")),
        name="mixer_gla",
    )(*args)


def _gla_c_kernel(q_ref, f_ref, i_ref, gt_ref, s0_ref, lbl_ref, hn_ref,
                  o_ref, sout_ref, st_s, *, L, HP, DK, DV, layer):
    ci = pl.program_id(2)

    @pl.when(ci == 0)
    def _():
        st_s[...] = s0_ref[0]

    lbl = lbl_ref[...]
    e = jnp.exp(lbl - jnp.max(lbl, axis=0, keepdims=True))
    sm = e / jnp.sum(e, axis=0, keepdims=True)
    lb = jnp.sum(sm[0:layer + 1, :], axis=0, keepdims=True) - sm[0:1, :]

    f = lb + (1.0 - lb) * _sigmoid(f_ref[...])
    b, worst = _gla_cum_decay(jnp.log(jnp.maximum(f, F_TINY)), L)

    def body(fact):
        for h in range(HP):
            sl = slice(h * DK, (h + 1) * DK)
            sv = slice(h * DV, (h + 1) * DV)
            o, st_new = _gla_chunk(_silu(q_ref[:, sl]), 1.0 - f[:, sl], i_ref[:, sv], b[:, sl],
                                   st_s[h], L, fact)
            st_s[h] = st_new
            o_ref[:, sv] = (_rms(o) * hn_ref[:, sv] * _silu(gt_ref[:, sv])).astype(o_ref.dtype)

    _either(worst > -SAFE_LOG, body)

    @pl.when(ci == pl.num_programs(2) - 1)
    def _():
        sout_ref[0] = st_s[...]


def _gla_c(srcs, s0t, lb_logits, hn, o_prev, *, layer, row0, B, T, L, NH, DK, DV, HP):
    nC = T // L
    rb0 = row0 // L
    (a_q, c_q), (a_f, c_f), (a_i, c_i), (a_g, c_g) = srcs
    wq, wv = HP * DK, HP * DV

    def rows(b, c):
        return rb0 + b * nC + c

    kern = functools.partial(_gla_c_kernel, L=L, HP=HP, DK=DK, DV=DV, layer=layer)
    in_specs = [pl.BlockSpec((L, wq), lambda b, g, c: (rows(b, c), c_q // wq + g)),
                pl.BlockSpec((L, wq), lambda b, g, c: (rows(b, c), c_f // wq + g)),
                pl.BlockSpec((L, wv), lambda b, g, c: (rows(b, c), c_i // wv + g)),
                pl.BlockSpec((L, wv), lambda b, g, c: (rows(b, c), c_g // wv + g)),
                pl.BlockSpec((1, HP, DV, DK), lambda b, g, c: (b, g, 0, 0)),
                pl.BlockSpec((lb_logits.shape[0], wq), lambda b, g, c: (0, g)),
                pl.BlockSpec((1, wv), lambda b, g, c: (0, g))]
    args = [a_q, a_f, a_i, a_g, s0t, lb_logits, hn]
    kern, in_specs, args, aliases = _alias_prev(kern, in_specs, args, o_prev)
    return pl.pallas_call(
        kern,
        grid=(B, NH // HP, nC),
        in_specs=in_specs,
        out_specs=[pl.BlockSpec((L, wv), lambda b, g, c: (rows(b, c), g)),
                   pl.BlockSpec((1, HP, DV, DK), lambda b, g, c: (b, g, 0, 0))],
        out_shape=[jax.ShapeDtypeStruct((a_q.shape[0], NH * DV), BF16),
                   jax.ShapeDtypeStruct((B, NH, DV, DK), F32)],
        scratch_shapes=[pltpu.VMEM((HP, DV, DK), F32)],
        input_output_aliases=aliases,
        compiler_params=_cparams(("arbitrary", "arbitrary", "arbitrary")),
        name="mixer_hgrn2",
    )(*args)


def _new_expert(be_ref):
    b = pl.program_id(1)
    return (b == 0) | (be_ref[b] != be_ref[jnp.maximum(b - 1, 0)])


def _round_chunks(chunk_refs, w_s):
    rc = chunk_refs[0].shape[0]
    for c, ref in enumerate(chunk_refs):
        w_s[c * rc:(c + 1) * rc, :] = ref[...].astype(BF16)


def _gmm1_kernel(be_ref, na_ref, x_ref, *refs):
    w1_refs, w3_refs = refs[:W_CHUNKS], refs[W_CHUNKS:2 * W_CHUNKS]
    o_ref, w1_s, w3_s = refs[2 * W_CHUNKS:]

    @pl.when(_new_expert(be_ref))
    def _():
        _round_chunks(w1_refs, w1_s)
        _round_chunks(w3_refs, w3_s)

    @pl.when(pl.program_id(1) < na_ref[0])
    def _():
        x_lo, x_hi = _unpack_bf16_pairs(x_ref[...])
        half = x_lo.shape[1]
        up = _dot(x_lo, w1_s[0:half, :]) + _dot(x_hi, w1_s[half:, :])
        gate = _dot(x_lo, w3_s[0:half, :]) + _dot(x_hi, w3_s[half:, :])
        o_ref[...] = (_silu(up) * gate).astype(o_ref.dtype)

    @pl.when(pl.program_id(1) >= na_ref[0])
    def _():
        o_ref[...] = jnp.zeros_like(o_ref)


def _gmm2_kernel(be_ref, na_ref, h_ref, *refs):
    w2_refs = refs[:W_CHUNKS]
    rw_ref, o_ref, w2_s = refs[W_CHUNKS:]

    @pl.when(_new_expert(be_ref))
    def _():
        _round_chunks(w2_refs, w2_s)

    @pl.when(pl.program_id(1) < na_ref[0])
    def _():
        o_ref[...] = _dot(h_ref[...], w2_s[...]) * rw_ref[...]

    @pl.when(pl.program_id(1) >= na_ref[0])
    def _():
        o_ref[...] = jnp.zeros_like(o_ref)


def _expert_mlp(xg, roww, blk_e, n_act, w1, w3, w2, layer):
    rows, dh = xg.shape
    d, f = w1.shape[2], w1.shape[3]
    r = MOE_ROWS
    nb = rows // r
    tf, td = f // 2, d // 2

    def act(b, na):
        return jnp.minimum(b, na[0] - 1)

    def chunks(nrows, ncols):
        rc = nrows // W_CHUNKS
        return [pl.BlockSpec((None, None, rc, ncols),
                             lambda j, b, be, na, c=c: (layer, be[b], c, j))
                for c in range(W_CHUNKS)]

    hb = pl.pallas_call(
        _gmm1_kernel,
        grid_spec=pltpu.PrefetchScalarGridSpec(
            num_scalar_prefetch=2, grid=(f // tf, nb),
            in_specs=[pl.BlockSpec((r, dh), lambda j, b, be, na: (act(b, na), 0))]
            + chunks(d, tf) + chunks(d, tf),
            out_specs=pl.BlockSpec((r, tf), lambda j, b, be, na: (b, j)),
            scratch_shapes=[pltpu.VMEM((d, tf), BF16), pltpu.VMEM((d, tf), BF16)]),
        out_shape=jax.ShapeDtypeStruct((rows, f), BF16),
        compiler_params=_cparams(("arbitrary", "arbitrary")),
        name="expert_up",
    )(blk_e, n_act, xg, *([w1] * W_CHUNKS), *([w3] * W_CHUNKS))
    return pl.pallas_call(
        _gmm2_kernel,
        grid_spec=pltpu.PrefetchScalarGridSpec(
            num_scalar_prefetch=2, grid=(d // td, nb),
            in_specs=[pl.BlockSpec((r, f), lambda j, b, be, na: (act(b, na), 0))]
            + chunks(f, td)
            + [pl.BlockSpec((r, 1), lambda j, b, be, na: (act(b, na), 0))],
            out_specs=pl.BlockSpec((r, td), lambda j, b, be, na: (b, j)),
            scratch_shapes=[pltpu.VMEM((f, td), BF16)]),
        out_shape=jax.ShapeDtypeStruct((rows, d), F32),
        compiler_params=_cparams(("arbitrary", "arbitrary")),
        name="expert_down",
    )(blk_e, n_act, hb, *([w2] * W_CHUNKS), roww)


def _route(logits):
    n = logits.shape[0]
    pg = jax.nn.softmax(logits[:, :N_GROUPS], axis=-1)
    grp = jnp.argmax(pg, axis=-1)
    p_grp = jnp.max(pg, axis=-1)
    le = logits[:, N_GROUPS:N_GROUPS + N_EXPERTS].reshape(n, N_GROUPS, EXPERTS_PER_GROUP)
    le_g = le[jnp.arange(n), grp]
    top_logit, top_j = lax.top_k(le_g, TOP_K)
    wts = jax.nn.softmax(top_logit, axis=-1) * p_grp[:, None]
    eid = (grp[:, None] * EXPERTS_PER_GROUP + top_j).astype(jnp.int32)
    return eid, wts


def _dispatch(eid, wts):
    n = eid.shape[0]
    a = n * TOP_K
    r = MOE_ROWS
    nb = -(-a // r) + N_EXPERTS
    flat_e = eid.reshape(a)
    onehot = (flat_e[:, None] == jnp.arange(N_EXPERTS, dtype=jnp.int32)[None, :]).astype(jnp.int32)
    seen = jnp.cumsum(onehot, axis=0)
    counts = seen[-1]
    padded = (counts + r - 1) // r * r
    pend = jnp.cumsum(padded)
    pstart = pend - padded
    dest = jnp.sum(onehot * (seen - 1 + pstart[None, :]), axis=1).astype(jnp.int32)
    rows = jnp.zeros((nb * r,), jnp.int32).at[dest].set(jnp.arange(a, dtype=jnp.int32) // TOP_K)
    roww = jnp.zeros((nb * r,), F32).at[dest].set(wts.reshape(a))
    n_act = (pend[-1] // r).astype(jnp.int32)
    blk = jnp.arange(nb, dtype=jnp.int32)
    blk_e = jnp.minimum(jnp.searchsorted(pend, jnp.minimum(blk, n_act - 1) * r, side='right'),
                        N_EXPERTS - 1).astype(jnp.int32)
    return rows, roww.reshape(nb * r, 1), blk_e, n_act.reshape(1), dest.reshape(n, TOP_K)


def kernel(x_prompt, x_sample, c_prompt, c_sample, state_a_C, state_a_n, state_a_m, state_a_conv, state_b_S, state_c_S, w_ada, b_ada, norm_mix, w_in, conv_w, conv_b, b_gate_a, hn_a, w_gk2, b_gk, hn_b, lb_logits, hn_c, w_br_a, w_br_b, w_br_c, w_out, norm_moe, w_rg, b_rg, w_re, b_re, w_exp1, w_exp3, w_exp2, norm_final):
    depth = w_ada.shape[0]
    bp, tp, d = x_prompt.shape
    bs, ts, _ = x_sample.shape
    nh_a, dv_a, dk_a = state_a_C.shape[2:]
    nh_b, dk_b, dv_b = state_b_S.shape[2:]
    nh_c, dk_c, dv_c = state_c_S.shape[2:]
    gate_rank = w_gk2.shape[1]
    qk_a, w_a = nh_a * dk_a, nh_a * dv_a
    qk_b, w_b = nh_b * dk_b, nh_b * dv_b
    qk_c, w_c = nh_c * dk_c, nh_c * dv_c
    mp, ms = bp * tp, bs * ts
    m = mp + ms
    assert bp == 1 and 1 + bs <= MOD_ROWS and m % TM_BIG == 0 and ms <= TM_SMALL

    sizes = (2 * qk_a, w_a, 2 * nh_a, w_a, qk_b, qk_b, w_b, gate_rank, w_b,
             qk_c, qk_c, w_c, w_c, N_BRANCH * d)
    names = ("a_qk", "a_v", "a_if", "a_o", "b_q", "b_k", "b_v", "b_lr", "b_g",
             "c_q", "c_f", "c_i", "c_g", "gates")
    src, o = {}, 0
    for nm, sz in zip(names, sizes):
        src[nm] = o
        o += sz
    n_src = o
    lr_lane = src["b_lr"] % LANES
    assert src["a_if"] % LANES == 0 and lr_lane + gate_rank <= LANES
    ranges = ((src["a_qk"], src["a_if"], ("a_qk", "a_v")),
              (src["a_o"], src["b_lr"], ("a_o", "b_q", "b_k", "b_v")),
              (src["b_g"], n_src, ("b_g", "c_q", "c_f", "c_i", "c_g", "gates")))
    where = {}
    for ri, (s0, s1, members) in enumerate(ranges):
        for nm in members:
            where[nm] = (ri, src[nm] - s0)
    gate_windows = (src["a_if"], src["b_lr"] - lr_lane)

    x = jnp.concatenate([x_prompt.reshape(mp, d), x_sample.reshape(ms, d)], axis=0)
    row_cond = jnp.concatenate([jnp.zeros((mp,), jnp.int32),
                                1 + jnp.arange(ms, dtype=jnp.int32) // ts])
    rsel = (row_cond[:, None] == jnp.arange(MOD_ROWS, dtype=jnp.int32)[None, :]).astype(BF16)
    c_all = jnp.concatenate([c_prompt, c_sample,
                             jnp.zeros((MOD_ROWS - bp - bs, d), F32)], axis=0)
    mod = _modulation(c_all, w_ada, b_ada)

    zeros = lambda *s: jnp.zeros(s, F32)
    pad_m = lambda mm: jnp.pad(mm, ((0, 0), (0, LANES - mm.shape[1])))[:, None, :]
    outs_p = [[] for _ in range(6)]
    outs_s = [[] for _ in range(6)]

    w_br_a16, w_br_b16, w_br_c16 = w_br_a.astype(BF16), w_br_b.astype(BF16), w_br_c.astype(BF16)
    w_out16 = w_out.astype(BF16)

    for l in range(depth):
        mod_l = mod[l:l + 1]
        h = _norm_mod(x, rsel, norm_mix[l], mod_l, 0, 1)
        ps = [_matmul(h, w_in[l, :, s0:s1].astype(BF16), F32, TM_BIG, TN_PROJ)
              for s0, s1, _ in ranges]
        w_gw = jnp.concatenate([w_in[l, :, s:s + LANES] for s in gate_windows], axis=1)
        p_gw = _matmul(h, w_gw.astype(BF16), F32, TM_BIG, 2 * LANES)

        def at(nm, extra=0):
            ri, c = where[nm]
            return ps[ri], c + extra

        bias_if = jnp.pad(b_gate_a[l].reshape(1, 2 * nh_a), ((0, 0), (0, LANES - 2 * nh_a)))
        w_gk2p = jnp.pad(w_gk2[l], ((lr_lane, LANES - lr_lane - gate_rank), (0, 0)))
        groups = (
            dict(row0=0, B=bp, T=tp, La=64, Lg=64,
                 conv0=zeros(bp, CONV_W - 1, 2 * qk_a), c0=zeros(bp, nh_a, dv_a, dk_a),
                 n0=zeros(bp, nh_a, dk_a), m0=zeros(bp, 1, LANES),
                 sb0=zeros(bp, nh_b, dv_b, dk_b), sc0=zeros(bp, nh_c, dv_c, dk_c)),
            dict(row0=mp, B=bs, T=ts, La=ts, Lg=ts,
                 conv0=state_a_conv[l], c0=state_a_C[l], n0=state_a_n[l],
                 m0=pad_m(state_a_m[l]),
                 sb0=jnp.swapaxes(state_b_S[l], -1, -2), sc0=jnp.swapaxes(state_c_S[l], -1, -2)),
        )
        o_a, o_b, o_c = (jnp.zeros((m, w), BF16) for w in (w_a, w_b, w_c))
        for g, outs in zip(groups, (outs_p, outs_s)):
            o_a, a_c, a_n, a_m, a_conv = _mlstm(
                (at("a_qk"), at("a_qk", qk_a), at("a_v"), at("a_o"), (p_gw, 0)),
                g["conv0"], g["c0"], g["n0"], g["m0"], conv_w[l], conv_b[l].reshape(1, -1),
                bias_if, hn_a[l].reshape(1, -1), o_a,
                row0=g["row0"], B=g["B"], T=g["T"], L=g["La"], NH=nh_a, DK=dk_a, DV=dv_a)
            o_b, b_st = _gla_b(
                (at("b_q"), at("b_k"), at("b_v"), at("b_g"), (p_gw, LANES)),
                g["sb0"], w_gk2p, b_gk[l].reshape(1, -1), hn_b[l].reshape(1, -1), o_b,
                row0=g["row0"], B=g["B"], T=g["T"], L=g["Lg"], NH=nh_b, DK=dk_b, DV=dv_b, HP=4)
            o_c, c_st = _gla_c(
                (at("c_q"), at("c_f"), at("c_i"), at("c_g")),
                g["sc0"], lb_logits, hn_c[l].reshape(1, -1), o_c,
                layer=l, row0=g["row0"], B=g["B"], T=g["T"], L=g["Lg"],
                NH=nh_c, DK=dk_c, DV=dv_c, HP=8)
            for lst, val in zip(outs, (a_c, a_n, a_m[:, 0, :nh_a], a_conv,
                                       jnp.swapaxes(b_st, -1, -2), jnp.swapaxes(c_st, -1, -2))):
                lst.append(val)

        merged = _merge(o_a, o_b, o_c, w_br_a16, w_br_b16, w_br_c16, l, *at("gates"))
        x = _out_proj(merged, w_out16, l, x, rsel, mod_l, 2)

        w_r = jnp.pad(jnp.concatenate([w_rg[l], w_re[l]], axis=1),
                      ((0, 0), (0, LANES - N_GROUPS - N_EXPERTS)))
        b_r = jnp.pad(jnp.concatenate([b_rg[l], b_re[l]]),
                      (0, LANES - N_GROUPS - N_EXPERTS)).reshape(1, LANES)
        h2, logits = _norm_mod_router(x, rsel, norm_moe[l], mod_l, 3, 4, w_r, b_r)
        eid, wts = _route(logits)
        rows, roww, blk_e, n_act, slot = _dispatch(eid, wts)
        yb = _expert_mlp(h2[rows], roww, blk_e, n_act, w_exp1, w_exp3, w_exp2, l)
        x = _gated_add(x, yb[slot[:, 0]], yb[slot[:, 1]], rsel, mod_l, 5)

    y_prompt = _final_norm(x, norm_final, 0, mp, 512).reshape(bp, tp, d)
    y_sample = _final_norm(x, norm_final, mp, ms, ms).reshape(bs, ts, d)
    stack = lambda lst: jnp.stack(lst)
    return (y_prompt, y_sample,
            stack(outs_p[0]), stack(outs_p[1]), stack(outs_p[2]), stack(outs_p[3]),
            stack(outs_p[4]), stack(outs_p[5]),
            stack(outs_s[0]), stack(outs_s[1]), stack(outs_s[2]), stack(outs_s[3]),
            stack(outs_s[4]), stack(outs_s[5]))
```

```python
import functools

import jax
import jax.numpy as jnp
import numpy as np
from jax import lax
from jax.experimental import pallas as pl
from jax.experimental.pallas import tpu as pltpu

F32 = jnp.float32
BF16 = jnp.bfloat16

NORM_EPS = 1e-6
NEG_BIG = -1e30
F_TINY = 1e-30
GLA_NORM = 16.0
CONV_W = 4
TOP_K = 2
N_GROUPS = 4
EXPERTS_PER_GROUP = 8
N_EXPERTS = N_GROUPS * EXPERTS_PER_GROUP
N_BRANCH = 3

LANES = 128
MOD_ROWS = 16
VMEM_LIMIT = 56 * 1024 * 1024

TM_BIG = 1040
TM_SMALL = 520
MOE_ROWS = 512
W_CHUNKS = 4
TN_PROJ = 1024
SUB = 16
SAFE_LOG = 60.0


def _cparams(sem):
    return pltpu.CompilerParams(dimension_semantics=sem, vmem_limit_bytes=VMEM_LIMIT)


def _dot(a, b):
    return jnp.dot(a, b, preferred_element_type=F32)


def _dot_nt(a, b):
    return lax.dot_general(a, b, (((1,), (1,)), ((), ())), preferred_element_type=F32)


def _dot_tn(a, b):
    return lax.dot_general(a, b, (((0,), (0,)), ((), ())), preferred_element_type=F32)


def _split3(x):
    hi = x.astype(BF16)
    r = x - hi.astype(F32)
    mid = r.astype(BF16)
    lo = (r - mid.astype(F32)).astype(BF16)
    return hi, mid, lo


def _dot_sel(sel, x):
    hi, mid, lo = _split3(x)
    return _dot(sel, hi) + _dot(sel, mid) + _dot(sel, lo)


def _dot_nt_sel(sel, x):
    hi, mid, lo = _split3(x)
    return _dot_nt(sel, hi) + _dot_nt(sel, mid) + _dot_nt(sel, lo)


def _dot_nt_f32(a, b):
    a_hi = a.astype(BF16)
    a_lo = (a - a_hi.astype(F32)).astype(BF16)
    b_hi = b.astype(BF16)
    b_lo = (b - b_hi.astype(F32)).astype(BF16)
    return _dot_nt(a_hi, b_hi) + _dot_nt(a_hi, b_lo) + _dot_nt(a_lo, b_hi)


def _r16(x):
    return x.astype(BF16).astype(F32)


def _pack_bf16_pairs(xb):
    n = xb.shape[1] // 2
    bits = pltpu.bitcast(xb.astype(F32), jnp.int32)
    return lax.shift_right_logical(bits[:, :n], 16) | bits[:, n:]


def _unpack_bf16_pairs(w):
    lo = pltpu.bitcast(lax.shift_left(w, 16), F32).astype(BF16)
    hi = pltpu.bitcast(w & jnp.int32(-65536), F32).astype(BF16)
    return lo, hi


def _sigmoid(x):
    return jax.nn.sigmoid(x)


def _silu(x):
    return x * jax.nn.sigmoid(x)


def _log_sigmoid(x):
    return jnp.minimum(x, 0.0) - jnp.log1p(jnp.exp(-jnp.abs(x)))


def _rms(x):
    return x * lax.rsqrt(jnp.mean(x * x, axis=-1, keepdims=True) + NORM_EPS)


def _tril(n):
    r = lax.broadcasted_iota(jnp.int32, (n, n), 0)
    c = lax.broadcasted_iota(jnp.int32, (n, n), 1)
    return jnp.where(r >= c, 1.0, 0.0).astype(BF16)


def _eye(n):
    r = lax.broadcasted_iota(jnp.int32, (n, n), 0)
    c = lax.broadcasted_iota(jnp.int32, (n, n), 1)
    return jnp.where(r == c, 1.0, 0.0).astype(BF16)


def _mod_kernel(c_ref, w_ref, b_ref, o_ref):
    c = c_ref[...]
    o_ref[0] = _dot(_silu(c).astype(BF16), w_ref[0].astype(BF16)) + b_ref[0]


def _modulation(c_all, w_ada, b_ada):
    depth, d, n = w_ada.shape
    tn = 1024
    return pl.pallas_call(
        _mod_kernel,
        grid=(depth, n // tn),
        in_specs=[pl.BlockSpec((MOD_ROWS, d), lambda l, j: (0, 0)),
                  pl.BlockSpec((1, d, tn), lambda l, j: (l, 0, j)),
                  pl.BlockSpec((1, 1, tn), lambda l, j: (l, 0, j))],
        out_specs=pl.BlockSpec((1, MOD_ROWS, tn), lambda l, j: (l, 0, j)),
        out_shape=jax.ShapeDtypeStruct((depth, MOD_ROWS, n), F32),
        compiler_params=_cparams(("arbitrary", "arbitrary")),
        name="modulation",
    )(c_all, w_ada, b_ada.reshape(depth, 1, n))


def _row_mod(mixed, r_ref, m_ref):
    if mixed:
        return _dot_sel(r_ref[...], m_ref[0])
    return m_ref[0, 0:1, :]


def _norm_mod_kernel(x_ref, r_ref, g_ref, sh_ref, sc_ref, o_ref):
    last = pl.num_programs(0) - 1

    def body(mixed):
        y = _rms(x_ref[...]) * g_ref[...]
        o_ref[...] = (y * (1.0 + _row_mod(mixed, r_ref, sc_ref)) + _row_mod(mixed, r_ref, sh_ref)
                      ).astype(o_ref.dtype)

    pl.when(pl.program_id(0) != last)(lambda: body(False))
    pl.when(pl.program_id(0) == last)(lambda: body(True))


def _norm_mod(x, rsel, gain, mod_l, sh_idx, sc_idx):
    m, d = x.shape
    tm = TM_SMALL
    return pl.pallas_call(
        _norm_mod_kernel,
        grid=(m // tm,),
        in_specs=[pl.BlockSpec((tm, d), lambda i: (i, 0)),
                  pl.BlockSpec((tm, MOD_ROWS), lambda i: (i, 0)),
                  pl.BlockSpec((1, d), lambda i: (0, 0)),
                  pl.BlockSpec((1, MOD_ROWS, d), lambda i: (0, 0, sh_idx)),
                  pl.BlockSpec((1, MOD_ROWS, d), lambda i: (0, 0, sc_idx))],
        out_specs=pl.BlockSpec((tm, d), lambda i: (i, 0)),
        out_shape=jax.ShapeDtypeStruct((m, d), BF16),
        compiler_params=_cparams(("arbitrary",)),
        name="norm_mod",
    )(x, rsel, gain.reshape(1, d), mod_l, mod_l)


def _norm_mod_router_kernel(x_ref, r_ref, g_ref, sh_ref, sc_ref, wr_ref, br_ref, o_ref, lg_ref):
    last = pl.num_programs(0) - 1

    def body(mixed):
        y = _rms(x_ref[...]) * g_ref[...]
        h = y * (1.0 + _row_mod(mixed, r_ref, sc_ref)) + _row_mod(mixed, r_ref, sh_ref)
        hb = h.astype(BF16)
        o_ref[...] = _pack_bf16_pairs(hb)
        lg_ref[...] = _dot(hb, wr_ref[...].astype(BF16)) + br_ref[...]

    pl.when(pl.program_id(0) != last)(lambda: body(False))
    pl.when(pl.program_id(0) == last)(lambda: body(True))


def _norm_mod_router(x, rsel, gain, mod_l, sh_idx, sc_idx, w_r, b_r):
    m, d = x.shape
    tm = TM_SMALL
    return pl.pallas_call(
        _norm_mod_router_kernel,
        grid=(m // tm,),
        in_specs=[pl.BlockSpec((tm, d), lambda i: (i, 0)),
                  pl.BlockSpec((tm, MOD_ROWS), lambda i: (i, 0)),
                  pl.BlockSpec((1, d), lambda i: (0, 0)),
                  pl.BlockSpec((1, MOD_ROWS, d), lambda i: (0, 0, sh_idx)),
                  pl.BlockSpec((1, MOD_ROWS, d), lambda i: (0, 0, sc_idx)),
                  pl.BlockSpec((d, LANES), lambda i: (0, 0)),
                  pl.BlockSpec((1, LANES), lambda i: (0, 0))],
        out_specs=[pl.BlockSpec((tm, d // 2), lambda i: (i, 0)),
                   pl.BlockSpec((tm, LANES), lambda i: (i, 0))],
        out_shape=[jax.ShapeDtypeStruct((m, d // 2), jnp.int32),
                   jax.ShapeDtypeStruct((m, LANES), F32)],
        compiler_params=_cparams(("arbitrary",)),
        name="norm_mod_router",
    )(x, rsel, gain.reshape(1, d), mod_l, mod_l, w_r, b_r)


def _final_norm_kernel(x_ref, g_ref, o_ref):
    o_ref[...] = _rms(x_ref[...]) * g_ref[...]


def _final_norm(x, gain, row0, rows, tm):
    d = x.shape[1]
    blk0 = row0 // tm
    return pl.pallas_call(
        _final_norm_kernel,
        grid=(rows // tm,),
        in_specs=[pl.BlockSpec((tm, d), lambda i: (blk0 + i, 0)),
                  pl.BlockSpec((1, d), lambda i: (0, 0))],
        out_specs=pl.BlockSpec((tm, d), lambda i: (i, 0)),
        out_shape=jax.ShapeDtypeStruct((rows, d), F32),
        compiler_params=_cparams(("arbitrary",)),
        name="final_norm",
    )(x, gain.reshape(1, d))


def _matmul_kernel(a_ref, b_ref, o_ref):
    o_ref[...] = _dot(a_ref[...], b_ref[...]).astype(o_ref.dtype)


def _matmul(a, b, out_dtype, tm, tn):
    m, k = a.shape
    n = b.shape[1]
    return pl.pallas_call(
        _matmul_kernel,
        grid=(m // tm, n // tn),
        in_specs=[pl.BlockSpec((tm, k), lambda i, j: (i, 0)),
                  pl.BlockSpec((k, tn), lambda i, j: (0, j))],
        out_specs=pl.BlockSpec((tm, tn), lambda i, j: (i, j)),
        out_shape=jax.ShapeDtypeStruct((m, n), out_dtype),
        compiler_params=_cparams(("arbitrary", "arbitrary")),
        name="in_proj",
    )(a, b)


def _merge_kernel(oa_ref, ob_ref, oc_ref, wa_ref, wb_ref, wc_ref, ga_ref, gb_ref, gc_ref, o_ref):
    acc = _sigmoid(ga_ref[...]) * _dot(oa_ref[...], wa_ref[...])
    acc = acc + _sigmoid(gb_ref[...]) * _dot(ob_ref[...], wb_ref[...])
    acc = acc + _sigmoid(gc_ref[...]) * _dot(oc_ref[...], wc_ref[...])
    o_ref[...] = acc.astype(o_ref.dtype)


def _merge(o_a, o_b, o_c, w_a, w_b, w_c, layer, p, gate_col0):
    m, kw = o_a.shape
    d = w_a.shape[2]
    tm, tn = TM_SMALL, 512
    g0 = gate_col0 // tn
    gstride = d // tn
    o_spec = pl.BlockSpec((tm, kw), lambda j, i: (i, 0))
    w_spec = pl.BlockSpec((None, kw, tn), lambda j, i: (layer, 0, j))

    def g_spec(br):
        return pl.BlockSpec((tm, tn), lambda j, i: (i, g0 + br * gstride + j))

    return pl.pallas_call(
        _merge_kernel,
        grid=(d // tn, m // tm),
        in_specs=[o_spec, o_spec, o_spec, w_spec, w_spec, w_spec, g_spec(0), g_spec(1), g_spec(2)],
        out_specs=pl.BlockSpec((tm, tn), lambda j, i: (i, j)),
        out_shape=jax.ShapeDtypeStruct((m, d), BF16),
        compiler_params=_cparams(("arbitrary", "arbitrary")),
        name="merge",
    )(o_a, o_b, o_c, w_a, w_b, w_c, p, p, p)


def _out_proj_kernel(a_ref, w_ref, x_ref, r_ref, g_ref, o_ref):
    last = pl.num_programs(0) - 1
    acc = _dot(a_ref[...], w_ref[...])

    def body(mixed):
        o_ref[...] = x_ref[...] + _row_mod(mixed, r_ref, g_ref) * acc

    pl.when(pl.program_id(0) != last)(lambda: body(False))
    pl.when(pl.program_id(0) == last)(lambda: body(True))


def _out_proj(a, w, layer, x, rsel, mod_l, g_idx):
    m, k = a.shape
    d = w.shape[2]
    tm, tn = TM_BIG, 512
    nj = d // tn
    return pl.pallas_call(
        _out_proj_kernel,
        grid=(m // tm, nj),
        in_specs=[pl.BlockSpec((tm, k), lambda i, j: (i, 0)),
                  pl.BlockSpec((None, k, tn), lambda i, j: (layer, 0, j)),
                  pl.BlockSpec((tm, tn), lambda i, j: (i, j)),
                  pl.BlockSpec((tm, MOD_ROWS), lambda i, j: (i, 0)),
                  pl.BlockSpec((1, MOD_ROWS, tn), lambda i, j: (0, 0, g_idx * nj + j))],
        out_specs=pl.BlockSpec((tm, tn), lambda i, j: (i, j)),
        out_shape=jax.ShapeDtypeStruct((m, d), F32),
        compiler_params=_cparams(("arbitrary", "arbitrary")),
        name="out_proj",
    )(a, w, x, rsel, mod_l)


def _gated_add_kernel(x_ref, y0_ref, y1_ref, r_ref, g_ref, o_ref):
    last = pl.num_programs(0) - 1

    def body(mixed):
        o_ref[...] = x_ref[...] + _row_mod(mixed, r_ref, g_ref) * (y0_ref[...] + y1_ref[...])

    pl.when(pl.program_id(0) != last)(lambda: body(False))
    pl.when(pl.program_id(0) == last)(lambda: body(True))


def _gated_add(x, y0, y1, rsel, mod_l, g_idx):
    m, d = x.shape
    tm, tn = TM_SMALL, 1024
    nj = d // tn
    spec = pl.BlockSpec((tm, tn), lambda i, j: (i, j))
    return pl.pallas_call(
        _gated_add_kernel,
        grid=(m // tm, nj),
        in_specs=[spec, spec, spec,
                  pl.BlockSpec((tm, MOD_ROWS), lambda i, j: (i, 0)),
                  pl.BlockSpec((1, MOD_ROWS, tn), lambda i, j: (0, 0, g_idx * nj + j))],
        out_specs=spec,
        out_shape=jax.ShapeDtypeStruct((m, d), F32),
        compiler_params=_cparams(("arbitrary", "arbitrary")),
        name="moe_combine",
    )(x, y0, y1, rsel, mod_l)


def _alias_prev(kern, in_specs, args, o_prev):
    if o_prev is None:
        return kern, in_specs, args, {}
    n = len(args)

    def wrapped(*refs):
        return kern(*refs[:n], *refs[n + 1:])

    return wrapped, in_specs + [pl.BlockSpec(memory_space=pl.ANY)], args + [o_prev], {n: 0}


def _mlstm_kernel(qp_ref, kp_ref, v_ref, og_ref, if_ref, conv0_ref, c0_ref, n0_ref, m0_ref,
                  cw_ref, cb_ref, bif_ref, hn_ref,
                  o_ref, cout_ref, nout_ref, mout_ref, convout_ref,
                  c_s, n_s, m_s, ubuf, q_s, k_s, *, L, NH, DK, DV):
    ci = pl.program_id(1)
    QK = NH * DK

    @pl.when(ci == 0)
    def _():
        c_s[...] = c0_ref[0]
        n_s[...] = n0_ref[0]
        m_s[...] = m0_ref[0]
        ubuf[8 - (CONV_W - 1):8, :] = conv0_ref[0]

    ubuf[8:8 + L, 0:QK] = qp_ref[...]
    ubuf[8:8 + L, QK:2 * QK] = kp_ref[...]
    y = cb_ref[...]
    for j in range(CONV_W):
        y = y + ubuf[8 - (CONV_W - 1) + j:8 - (CONV_W - 1) + j + L, :] * cw_ref[j:j + 1, :]
    ubuf[0:8, :] = ubuf[L:L + 8, :]
    qk = _silu(y)
    q_s[...] = qk[:, 0:QK] * (DK ** -0.5)
    k_s[...] = qk[:, QK:2 * QK]

    ifv = if_ref[...] + bif_ref[...]
    b_all = _dot_sel(_tril(L), _log_sigmoid(ifv))
    eye = _eye(LANES)
    ig_t = _dot_nt_sel(eye, ifv)
    b_t = _dot_nt_sel(eye, b_all)
    row = lax.broadcasted_iota(jnp.int32, (L, L), 0)
    col = lax.broadcasted_iota(jnp.int32, (L, L), 1)
    causal = row >= col
    lane = lax.broadcasted_iota(jnp.int32, (1, LANES), 1)
    m_old = m_s[...]
    m_new = m_old

    for h in range(NH):
        b_c = b_all[:, NH + h:NH + h + 1]
        ig_c = ifv[:, h:h + 1]
        b_r = b_t[NH + h:NH + h + 1, :]
        ig_r = ig_t[h:h + 1, :]
        m_prev = m_old[:, h:h + 1]
        log_d = jnp.where(causal, b_c - b_r + ig_r, NEG_BIG)
        inter = b_c + m_prev
        mt = jnp.maximum(jnp.max(log_d, axis=-1, keepdims=True), inter)
        dm = jnp.exp(log_d - mt)
        sc = jnp.exp(inter - mt)
        qh = q_s[:, h * DK:(h + 1) * DK]
        kh = k_s[:, h * DK:(h + 1) * DK]
        vh = v_ref[:, h * DV:(h + 1) * DV]
        qb, kb, vb = qh.astype(BF16), kh.astype(BF16), vh.astype(BF16)
        c_h = c_s[h]
        n_h = n_s[h:h + 1, :]
        s = _dot_nt(qb, kb) * dm
        num = sc * _dot_nt(qb, c_h.astype(BF16)) + _dot(s.astype(BF16), vb)
        den = (sc * jnp.sum(qb.astype(F32) * _r16(n_h), axis=-1, keepdims=True)
               + jnp.sum(s, axis=-1, keepdims=True))
        hc = num / jnp.maximum(jnp.abs(den), jnp.exp(-mt))
        mt_l = mt[L - 1:L, :]
        w_l = jnp.exp((b_c[L - 1:L, :] - b_c) + ig_c - mt_l)
        s_l = sc[L - 1:L, :]
        c_s[h] = s_l * c_h + _dot_tn(vb, (kh * w_l).astype(BF16))
        n_s[h:h + 1, :] = s_l * n_h + jnp.sum(_r16(w_l) * kb.astype(F32), axis=0, keepdims=True)
        m_new = jnp.where(lane == h, mt_l, m_new)
        out = _rms(hc) * hn_ref[:, h * DV:(h + 1) * DV] * _sigmoid(og_ref[:, h * DV:(h + 1) * DV])
        o_ref[:, h * DV:(h + 1) * DV] = out.astype(o_ref.dtype)

    m_s[...] = m_new

    @pl.when(ci == pl.num_programs(1) - 1)
    def _():
        cout_ref[0] = c_s[...]
        nout_ref[0] = n_s[...]
        mout_ref[0] = m_s[...]
        convout_ref[0] = ubuf[8 - (CONV_W - 1):8, :]


def _mlstm(srcs, conv0, c0, n0, m0, conv_w, conv_b, bias_if, hn, o_prev,
           *, row0, B, T, L, NH, DK, DV):
    QK, W = NH * DK, NH * DV
    nC = T // L
    rb0 = row0 // L
    (a_q, c_q), (a_k, c_k), (a_v, c_v), (a_o, c_o), (a_if, c_if) = srcs

    def rows(b, c):
        return rb0 + b * nC + c

    kern = functools.partial(_mlstm_kernel, L=L, NH=NH, DK=DK, DV=DV)
    in_specs = [pl.BlockSpec((L, QK), lambda b, c: (rows(b, c), c_q // QK)),
                  pl.BlockSpec((L, QK), lambda b, c: (rows(b, c), c_k // QK)),
                  pl.BlockSpec((L, W), lambda b, c: (rows(b, c), c_v // W)),
                  pl.BlockSpec((L, W), lambda b, c: (rows(b, c), c_o // W)),
                  pl.BlockSpec((L, LANES), lambda b, c: (rows(b, c), c_if // LANES)),
                  pl.BlockSpec((1, CONV_W - 1, 2 * QK), lambda b, c: (b, 0, 0)),
                  pl.BlockSpec((1, NH, DV, DK), lambda b, c: (b, 0, 0, 0)),
                  pl.BlockSpec((1, NH, DK), lambda b, c: (b, 0, 0)),
                  pl.BlockSpec((1, 1, LANES), lambda b, c: (b, 0, 0)),
                  pl.BlockSpec((CONV_W, 2 * QK), lambda b, c: (0, 0)),
                  pl.BlockSpec((1, 2 * QK), lambda b, c: (0, 0)),
                  pl.BlockSpec((1, LANES), lambda b, c: (0, 0)),
                  pl.BlockSpec((1, W), lambda b, c: (0, 0))]
    args = [a_q, a_k, a_v, a_o, a_if, conv0, c0, n0, m0, conv_w, conv_b, bias_if, hn]
    kern, in_specs, args, aliases = _alias_prev(kern, in_specs, args, o_prev)
    return pl.pallas_call(
        kern,
        grid=(B, nC),
        in_specs=in_specs,
        out_specs=[pl.BlockSpec((L, W), lambda b, c: (rows(b, c), 0)),
                   pl.BlockSpec((1, NH, DV, DK), lambda b, c: (b, 0, 0, 0)),
                   pl.BlockSpec((1, NH, DK), lambda b, c: (b, 0, 0)),
                   pl.BlockSpec((1, 1, LANES), lambda b, c: (b, 0, 0)),
                   pl.BlockSpec((1, CONV_W - 1, 2 * QK), lambda b, c: (b, 0, 0))],
        out_shape=[jax.ShapeDtypeStruct((a_q.shape[0], W), BF16),
                   jax.ShapeDtypeStruct((B, NH, DV, DK), F32),
                   jax.ShapeDtypeStruct((B, NH, DK), F32),
                   jax.ShapeDtypeStruct((B, 1, LANES), F32),
                   jax.ShapeDtypeStruct((B, CONV_W - 1, 2 * QK), F32)],
        scratch_shapes=[pltpu.VMEM((NH, DV, DK), F32),
                        pltpu.VMEM((NH, DK), F32),
                        pltpu.VMEM((1, LANES), F32),
                        pltpu.VMEM((L + 8, 2 * QK), F32),
                        pltpu.VMEM((L, QK), F32),
                        pltpu.VMEM((L, QK), F32)],
        input_output_aliases=aliases,
        compiler_params=_cparams(("arbitrary", "arbitrary")),
        name="mixer_mlstm",
    )(*args)


def _gla_intra_exact(q, k, b, L, c):
    dk = q.shape[1]
    nsub = L // c
    q3 = q.reshape(nsub, c, dk)
    k3 = k.reshape(nsub, c, dk)
    b3 = b.reshape(nsub, c, dk)
    t_idx = lax.broadcasted_iota(jnp.int32, (1, c, 1), 1)
    s_idx = lax.broadcasted_iota(jnp.int32, (1, 1, c), 2)
    a_diag = jnp.zeros((nsub, c, c), F32)
    for s in range(c):
        arg = jnp.where(t_idx >= s, b3 - b3[:, s:s + 1, :], NEG_BIG)
        col_s = jnp.sum(q3 * k3[:, s:s + 1, :] * jnp.exp(arg), axis=-1, keepdims=True)
        a_diag = jnp.where(s_idx == s, col_s, a_diag)
    a_diag = a_diag.reshape(L, c)
    if nsub == 1:
        return a_diag
    row = lax.broadcasted_iota(jnp.int32, (L, L), 0)
    col = lax.broadcasted_iota(jnp.int32, (L, L), 1)
    rep_r = lax.broadcasted_iota(jnp.int32, (c, L), 0)
    rep_c = lax.broadcasted_iota(jnp.int32, (c, L), 1)
    rep = jnp.where((rep_c & (c - 1)) == rep_r, 1.0, 0.0).astype(BF16)
    a = jnp.where((row & -c) == (col & -c), _dot(a_diag.astype(BF16), rep), 0.0)
    blocks = [jnp.zeros((c, L), F32)]
    for i in range(1, nsub):
        r_i = b[i * c - 1:i * c, :]
        q_i = q[i * c:(i + 1) * c, :] * jnp.exp(b[i * c:(i + 1) * c, :] - r_i)
        k_i = k * jnp.exp(jnp.minimum(r_i - b, 0.0))
        blocks.append(_dot_nt_f32(q_i, k_i))
    return a + jnp.where((col & -c) < (row & -c), jnp.concatenate(blocks, axis=0), 0.0)


def _rows_of(vecs, c):
    n = vecs[0].shape[1]
    parts = [jnp.broadcast_to(v, (c, n)) for v in vecs]
    return parts[0] if len(parts) == 1 else jnp.concatenate(parts, axis=0)


def _gla_chunk_fact(q, k, v, b, st, L):
    c = min(SUB, L)
    nsub = L // c
    dk = q.shape[1]
    r = ([jnp.zeros((1, dk), F32)] + [b[j * c - 1:j * c, :] for j in range(1, nsub)]
         + [b[L - 1:L, :]])
    q_t = q * jnp.exp(b - _rows_of(r[:nsub], c))
    k_h = k * jnp.exp(_rows_of(r[1:], c) - b)
    o = _dot_nt((q_t * _rows_of([jnp.exp(rj) for rj in r[:nsub]], c)).astype(BF16),
                st.astype(BF16))
    row = lax.broadcasted_iota(jnp.int32, (L, L), 0)
    col = lax.broadcasted_iota(jnp.int32, (L, L), 1)
    blocks = []
    for i in range(nsub):
        scale = _rows_of([jnp.exp(r[i] - r[j + 1]) if j <= i else jnp.ones((1, dk), F32)
                          for j in range(nsub)], c)
        blocks.append(_dot_nt_f32(q_t[i * c:(i + 1) * c, :], k_h * scale))
    a = blocks[0] if nsub == 1 else jnp.concatenate(blocks, axis=0)
    a = jnp.where(col <= row, a, 0.0)
    vb = v.astype(BF16)
    o = o + _dot(a.astype(BF16), vb)
    k_l = k_h * _rows_of([jnp.exp(r[nsub] - rj) for rj in r[1:]], c)
    st_new = st * jnp.exp(r[nsub]) + _dot_tn(vb, k_l.astype(BF16))
    return o, st_new


def _gla_cum_decay(g, L):
    c = min(SUB, L)
    nsub = L // c
    b = _dot_sel(_tril(L), g)
    drops = [b[c - 1:c, :]] + [b[(i + 1) * c - 1:(i + 1) * c, :] - b[i * c - 1:i * c, :]
                               for i in range(1, nsub)]
    return b, jnp.min(drops[0] if nsub == 1 else jnp.concatenate(drops, axis=0))


def _either(pred, body):
    pl.when(pred)(lambda: body(True))
    pl.when(jnp.logical_not(pred))(lambda: body(False))


def _gla_chunk(q, k, v, b, st, L, fact):
    if fact:
        return _gla_chunk_fact(q, k, v, b, st, L)
    c = min(SUB, L)
    b_l = b[L - 1:L, :]
    o = _dot_nt((q * jnp.exp(b)).astype(BF16), st.astype(BF16))
    a = _gla_intra_exact(q, k, b, L, c)
    vb = v.astype(BF16)
    o = o + _dot(a.astype(BF16), vb)
    st_new = st * jnp.exp(b_l) + _dot_tn(vb, (k * jnp.exp(b_l - b)).astype(BF16))
    return o, st_new


def _gla_b_kernel(q_ref, k_ref, v_ref, gt_ref, lr_ref, s0_ref, w2_ref, bgk_ref, hn_ref,
                  o_ref, sout_ref, st_s, *, L, HP, DK, DV):
    ci = pl.program_id(2)

    @pl.when(ci == 0)
    def _():
        st_s[...] = s0_ref[0]

    z = _dot(lr_ref[...].astype(BF16), w2_ref[...].astype(BF16)) + bgk_ref[...]
    b, worst = _gla_cum_decay(_log_sigmoid(z) / GLA_NORM, L)

    def body(fact):
        for h in range(HP):
            sl = slice(h * DK, (h + 1) * DK)
            sv = slice(h * DV, (h + 1) * DV)
            o, st_new = _gla_chunk(q_ref[:, sl] * (DK ** -0.5), k_ref[:, sl], v_ref[:, sv],
                                   b[:, sl], st_s[h], L, fact)
            st_s[h] = st_new
            o_ref[:, sv] = (_rms(o) * hn_ref[:, sv] * _silu(gt_ref[:, sv])).astype(o_ref.dtype)

    _either(worst > -SAFE_LOG, body)

    @pl.when(ci == pl.num_programs(2) - 1)
    def _():
        sout_ref[0] = st_s[...]


def _gla_b(srcs, s0t, w_gk2p, b_gk, hn, o_prev, *, row0, B, T, L, NH, DK, DV, HP):
    nC = T // L
    rb0 = row0 // L
    (a_q, c_q), (a_k, c_k), (a_v, c_v), (a_g, c_g), (a_lr, c_lr) = srcs
    wq, wv = HP * DK, HP * DV

    def rows(b, c):
        return rb0 + b * nC + c

    kern = functools.partial(_gla_b_kernel, L=L, HP=HP, DK=DK, DV=DV)
    in_specs = [pl.BlockSpec((L, wq), lambda b, g, c: (rows(b, c), c_q // wq + g)),
                pl.BlockSpec((L, wq), lambda b, g, c: (rows(b, c), c_k // wq + g)),
                pl.BlockSpec((L, wv), lambda b, g, c: (rows(b, c), c_v // wv + g)),
                pl.BlockSpec((L, wv), lambda b, g, c: (rows(b, c), c_g // wv + g)),
                pl.BlockSpec((L, LANES), lambda b, g, c: (rows(b, c), c_lr // LANES)),
                pl.BlockSpec((1, HP, DV, DK), lambda b, g, c: (b, g, 0, 0)),
                pl.BlockSpec((LANES, wq), lambda b, g, c: (0, g)),
                pl.BlockSpec((1, wq), lambda b, g, c: (0, g)),
                pl.BlockSpec((1, wv), lambda b, g, c: (0, g))]
    args = [a_q, a_k, a_v, a_g, a_lr, s0t, w_gk2p, b_gk, hn]
    kern, in_specs, args, aliases = _alias_prev(kern, in_specs, args, o_prev)
    return pl.pallas_call(
        kern,
        grid=(B, NH // HP, nC),
        in_specs=in_specs,
        out_specs=[pl.BlockSpec((L, wv), lambda b, g, c: (rows(b, c), g)),
                   pl.BlockSpec((1, HP, DV, DK), lambda b, g, c: (b, g, 0, 0))],
        out_shape=[jax.ShapeDtypeStruct((a_q.shape[0], NH * DV), BF16),
                   jax.ShapeDtypeStruct((B, NH, DV, DK), F32)],
        scratch_shapes=[pltpu.VMEM((HP, DV, DK), F32)],
        input_output_aliases=aliases,
        compiler_params=_cparams(("arbitrary", "arbitrary", "arbitrary")),
        name="mixer_gla",
    )(*args)


def _gla_c_kernel(q_ref, f_ref, i_ref, gt_ref, s0_ref, lbl_ref, hn_ref,
                  o_ref, sout_ref, st_s, *, L, HP, DK, DV, layer):
    ci = pl.program_id(2)

    @pl.when(ci == 0)
    def _():
        st_s[...] = s0_ref[0]

    lbl = lbl_ref[...]
    e = jnp.exp(lbl - jnp.max(lbl, axis=0, keepdims=True))
    sm = e / jnp.sum(e, axis=0, keepdims=True)
    lb = jnp.sum(sm[0:layer + 1, :], axis=0, keepdims=True) - sm[0:1, :]

    f = lb + (1.0 - lb) * _sigmoid(f_ref[...])
    b, worst = _gla_cum_decay(jnp.log(jnp.maximum(f, F_TINY)), L)

    def body(fact):
        for h in range(HP):
            sl = slice(h * DK, (h + 1) * DK)
            sv = slice(h * DV, (h + 1) * DV)
            o, st_new = _gla_chunk(_silu(q_ref[:, sl]), 1.0 - f[:, sl], i_ref[:, sv], b[:, sl],
                                   st_s[h], L, fact)
            st_s[h] = st_new
            o_ref[:, sv] = (_rms(o) * hn_ref[:, sv] * _silu(gt_ref[:, sv])).astype(o_ref.dtype)

    _either(worst > -SAFE_LOG, body)

    @pl.when(ci == pl.num_programs(2) - 1)
    def _():
        sout_ref[0] = st_s[...]


def _gla_c(srcs, s0t, lb_logits, hn, o_prev, *, layer, row0, B, T, L, NH, DK, DV, HP):
    nC = T // L
    rb0 = row0 // L
    (a_q, c_q), (a_f, c_f), (a_i, c_i), (a_g, c_g) = srcs
    wq, wv = HP * DK, HP * DV

    def rows(b, c):
        return rb0 + b * nC + c

    kern = functools.partial(_gla_c_kernel, L=L, HP=HP, DK=DK, DV=DV, layer=layer)
    in_specs = [pl.BlockSpec((L, wq), lambda b, g, c: (rows(b, c), c_q // wq + g)),
                pl.BlockSpec((L, wq), lambda b, g, c: (rows(b, c), c_f // wq + g)),
                pl.BlockSpec((L, wv), lambda b, g, c: (rows(b, c), c_i // wv + g)),
                pl.BlockSpec((L, wv), lambda b, g, c: (rows(b, c), c_g // wv + g)),
                pl.BlockSpec((1, HP, DV, DK), lambda b, g, c: (b, g, 0, 0)),
                pl.BlockSpec((lb_logits.shape[0], wq), lambda b, g, c: (0, g)),
                pl.BlockSpec((1, wv), lambda b, g, c: (0, g))]
    args = [a_q, a_f, a_i, a_g, s0t, lb_logits, hn]
    kern, in_specs, args, aliases = _alias_prev(kern, in_specs, args, o_prev)
    return pl.pallas_call(
        kern,
        grid=(B, NH // HP, nC),
        in_specs=in_specs,
        out_specs=[pl.BlockSpec((L, wv), lambda b, g, c: (rows(b, c), g)),
                   pl.BlockSpec((1, HP, DV, DK), lambda b, g, c: (b, g, 0, 0))],
        out_shape=[jax.ShapeDtypeStruct((a_q.shape[0], NH * DV), BF16),
                   jax.ShapeDtypeStruct((B, NH, DV, DK), F32)],
        scratch_shapes=[pltpu.VMEM((HP, DV, DK), F32)],
        input_output_aliases=aliases,
        compiler_params=_cparams(("arbitrary", "arbitrary", "arbitrary")),
        name="mixer_hgrn2",
    )(*args)


def _new_expert(be_ref):
    b = pl.program_id(1)
    return (b == 0) | (be_ref[b] != be_ref[jnp.maximum(b - 1, 0)])


def _round_chunks(chunk_refs, w_s):
    rc = chunk_refs[0].shape[0]
    for c, ref in enumerate(chunk_refs):
        w_s[c * rc:(c + 1) * rc, :] = ref[...].astype(BF16)


def _gmm1_kernel(be_ref, na_ref, x_ref, *refs):
    w1_refs, w3_refs = refs[:W_CHUNKS], refs[W_CHUNKS:2 * W_CHUNKS]
    o_ref, w1_s, w3_s = refs[2 * W_CHUNKS:]

    @pl.when(_new_expert(be_ref))
    def _():
        _round_chunks(w1_refs, w1_s)
        _round_chunks(w3_refs, w3_s)

    @pl.when(pl.program_id(1) < na_ref[0])
    def _():
        x_lo, x_hi = _unpack_bf16_pairs(x_ref[...])
        half = x_lo.shape[1]
        up = _dot(x_lo, w1_s[0:half, :]) + _dot(x_hi, w1_s[half:, :])
        gate = _dot(x_lo, w3_s[0:half, :]) + _dot(x_hi, w3_s[half:, :])
        o_ref[...] = (_silu(up) * gate).astype(o_ref.dtype)

    @pl.when(pl.program_id(1) >= na_ref[0])
    def _():
        o_ref[...] = jnp.zeros_like(o_ref)


def _gmm2_kernel(be_ref, na_ref, h_ref, *refs):
    w2_refs = refs[:W_CHUNKS]
    rw_ref, o_ref, w2_s = refs[W_CHUNKS:]

    @pl.when(_new_expert(be_ref))
    def _():
        _round_chunks(w2_refs, w2_s)

    @pl.when(pl.program_id(1) < na_ref[0])
    def _():
        o_ref[...] = _dot(h_ref[...], w2_s[...]) * rw_ref[...]

    @pl.when(pl.program_id(1) >= na_ref[0])
    def _():
        o_ref[...] = jnp.zeros_like(o_ref)


def _expert_mlp(xg, roww, blk_e, n_act, w1, w3, w2, layer):
    rows, dh = xg.shape
    d, f = w1.shape[2], w1.shape[3]
    r = MOE_ROWS
    nb = rows // r
    tf, td = f // 2, d // 2

    def act(b, na):
        return jnp.minimum(b, na[0] - 1)

    def chunks(nrows, ncols):
        rc = nrows // W_CHUNKS
        return [pl.BlockSpec((None, None, rc, ncols),
                             lambda j, b, be, na, c=c: (layer, be[b], c, j))
                for c in range(W_CHUNKS)]

    hb = pl.pallas_call(
        _gmm1_kernel,
        grid_spec=pltpu.PrefetchScalarGridSpec(
            num_scalar_prefetch=2, grid=(f // tf, nb),
            in_specs=[pl.BlockSpec((r, dh), lambda j, b, be, na: (act(b, na), 0))]
            + chunks(d, tf) + chunks(d, tf),
            out_specs=pl.BlockSpec((r, tf), lambda j, b, be, na: (b, j)),
            scratch_shapes=[pltpu.VMEM((d, tf), BF16), pltpu.VMEM((d, tf), BF16)]),
        out_shape=jax.ShapeDtypeStruct((rows, f), BF16),
        compiler_params=_cparams(("arbitrary", "arbitrary")),
        name="expert_up",
    )(blk_e, n_act, xg, *([w1] * W_CHUNKS), *([w3] * W_CHUNKS))
    return pl.pallas_call(
        _gmm2_kernel,
        grid_spec=pltpu.PrefetchScalarGridSpec(
            num_scalar_prefetch=2, grid=(d // td, nb),
            in_specs=[pl.BlockSpec((r, f), lambda j, b, be, na: (act(b, na), 0))]
            + chunks(f, td)
            + [pl.BlockSpec((r, 1), lambda j, b, be, na: (act(b, na), 0))],
            out_specs=pl.BlockSpec((r, td), lambda j, b, be, na: (b, j)),
            scratch_shapes=[pltpu.VMEM((f, td), BF16)]),
        out_shape=jax.ShapeDtypeStruct((rows, d), F32),
        compiler_params=_cparams(("arbitrary", "arbitrary")),
        name="expert_down",
    )(blk_e, n_act, hb, *([w2] * W_CHUNKS), roww)


def _route(logits):
    n = logits.shape[0]
    pg = jax.nn.softmax(logits[:, :N_GROUPS], axis=-1)
    grp = jnp.argmax(pg, axis=-1)
    p_grp = jnp.max(pg, axis=-1)
    le = logits[:, N_GROUPS:N_GROUPS + N_EXPERTS].reshape(n, N_GROUPS, EXPERTS_PER_GROUP)
    le_g = le[jnp.arange(n), grp]
    top_logit, top_j = lax.top_k(le_g, TOP_K)
    wts = jax.nn.softmax(top_logit, axis=-1) * p_grp[:, None]
    eid = (grp[:, None] * EXPERTS_PER_GROUP + top_j).astype(jnp.int32)
    return eid, wts


def _dispatch(eid, wts):
    n = eid.shape[0]
    a = n * TOP_K
    r = MOE_ROWS
    nb = -(-a // r) + N_EXPERTS
    flat_e = eid.reshape(a)
    onehot = (flat_e[:, None] == jnp.arange(N_EXPERTS, dtype=jnp.int32)[None, :]).astype(jnp.int32)
    seen = jnp.cumsum(onehot, axis=0)
    counts = seen[-1]
    padded = (counts + r - 1) // r * r
    pend = jnp.cumsum(padded)
    pstart = pend - padded
    dest = jnp.sum(onehot * (seen - 1 + pstart[None, :]), axis=1).astype(jnp.int32)
    rows = jnp.zeros((nb * r,), jnp.int32).at[dest].set(jnp.arange(a, dtype=jnp.int32) // TOP_K)
    roww = jnp.zeros((nb * r,), F32).at[dest].set(wts.reshape(a))
    n_act = (pend[-1] // r).astype(jnp.int32)
    blk = jnp.arange(nb, dtype=jnp.int32)
    blk_e = jnp.minimum(jnp.searchsorted(pend, jnp.minimum(blk, n_act - 1) * r, side='right'),
                        N_EXPERTS - 1).astype(jnp.int32)
    return rows, roww.reshape(nb * r, 1), blk_e, n_act.reshape(1), dest.reshape(n, TOP_K)


def kernel(x_prompt, x_sample, c_prompt, c_sample, state_a_C, state_a_n, state_a_m, state_a_conv, state_b_S, state_c_S, w_ada, b_ada, norm_mix, w_in, conv_w, conv_b, b_gate_a, hn_a, w_gk2, b_gk, hn_b, lb_logits, hn_c, w_br_a, w_br_b, w_br_c, w_out, norm_moe, w_rg, b_rg, w_re, b_re, w_exp1, w_exp3, w_exp2, norm_final):
    depth = w_ada.shape[0]
    bp, tp, d = x_prompt.shape
    bs, ts, _ = x_sample.shape
    nh_a, dv_a, dk_a = state_a_C.shape[2:]
    nh_b, dk_b, dv_b = state_b_S.shape[2:]
    nh_c, dk_c, dv_c = state_c_S.shape[2:]
    gate_rank = w_gk2.shape[1]
    qk_a, w_a = nh_a * dk_a, nh_a * dv_a
    qk_b, w_b = nh_b * dk_b, nh_b * dv_b
    qk_c, w_c = nh_c * dk_c, nh_c * dv_c
    mp, ms = bp * tp, bs * ts
    m = mp + ms
    assert bp == 1 and 1 + bs <= MOD_ROWS and m % TM_BIG == 0 and ms <= TM_SMALL

    sizes = (2 * qk_a, w_a, 2 * nh_a, w_a, qk_b, qk_b, w_b, gate_rank, w_b,
             qk_c, qk_c, w_c, w_c, N_BRANCH * d)
    names = ("a_qk", "a_v", "a_if", "a_o", "b_q", "b_k", "b_v", "b_lr", "b_g",
             "c_q", "c_f", "c_i", "c_g", "gates")
    src, o = {}, 0
    for nm, sz in zip(names, sizes):
        src[nm] = o
        o += sz
    n_src = o
    lr_lane = src["b_lr"] % LANES
    assert src["a_if"] % LANES == 0 and lr_lane + gate_rank <= LANES
    ranges = ((src["a_qk"], src["a_if"], ("a_qk", "a_v")),
              (src["a_o"], src["b_lr"], ("a_o", "b_q", "b_k", "b_v")),
              (src["b_g"], n_src, ("b_g", "c_q", "c_f", "c_i", "c_g", "gates")))
    where = {}
    for ri, (s0, s1, members) in enumerate(ranges):
        for nm in members:
            where[nm] = (ri, src[nm] - s0)
    gate_windows = (src["a_if"], src["b_lr"] - lr_lane)

    x = jnp.concatenate([x_prompt.reshape(mp, d), x_sample.reshape(ms, d)], axis=0)
    row_cond = jnp.concatenate([jnp.zeros((mp,), jnp.int32),
                                1 + jnp.arange(ms, dtype=jnp.int32) // ts])
    rsel = (row_cond[:, None] == jnp.arange(MOD_ROWS, dtype=jnp.int32)[None, :]).astype(BF16)
    c_all = jnp.concatenate([c_prompt, c_sample,
                             jnp.zeros((MOD_ROWS - bp - bs, d), F32)], axis=0)
    mod = _modulation(c_all, w_ada, b_ada)

    zeros = lambda *s: jnp.zeros(s, F32)
    pad_m = lambda mm: jnp.pad(mm, ((0, 0), (0, LANES - mm.shape[1])))[:, None, :]
    outs_p = [[] for _ in range(6)]
    outs_s = [[] for _ in range(6)]

    w_br_a16, w_br_b16, w_br_c16 = w_br_a.astype(BF16), w_br_b.astype(BF16), w_br_c.astype(BF16)
    w_out16 = w_out.astype(BF16)

    for l in range(depth):
        mod_l = mod[l:l + 1]
        h = _norm_mod(x, rsel, norm_mix[l], mod_l, 0, 1)
        ps = [_matmul(h, w_in[l, :, s0:s1].astype(BF16), F32, TM_BIG, TN_PROJ)
              for s0, s1, _ in ranges]
        w_gw = jnp.concatenate([w_in[l, :, s:s + LANES] for s in gate_windows], axis=1)
        p_gw = _matmul(h, w_gw.astype(BF16), F32, TM_BIG, 2 * LANES)

        def at(nm, extra=0):
            ri, c = where[nm]
            return ps[ri], c + extra

        bias_if = jnp.pad(b_gate_a[l].reshape(1, 2 * nh_a), ((0, 0), (0, LANES - 2 * nh_a)))
        w_gk2p = jnp.pad(w_gk2[l], ((lr_lane, LANES - lr_lane - gate_rank), (0, 0)))
        groups = (
            dict(row0=0, B=bp, T=tp, La=64, Lg=64,
                 conv0=zeros(bp, CONV_W - 1, 2 * qk_a), c0=zeros(bp, nh_a, dv_a, dk_a),
                 n0=zeros(bp, nh_a, dk_a), m0=zeros(bp, 1, LANES),
                 sb0=zeros(bp, nh_b, dv_b, dk_b), sc0=zeros(bp, nh_c, dv_c, dk_c)),
            dict(row0=mp, B=bs, T=ts, La=ts, Lg=ts,
                 conv0=state_a_conv[l], c0=state_a_C[l], n0=state_a_n[l],
                 m0=pad_m(state_a_m[l]),
                 sb0=jnp.swapaxes(state_b_S[l], -1, -2), sc0=jnp.swapaxes(state_c_S[l], -1, -2)),
        )
        o_a, o_b, o_c = (jnp.zeros((m, w), BF16) for w in (w_a, w_b, w_c))
        for g, outs in zip(groups, (outs_p, outs_s)):
            o_a, a_c, a_n, a_m, a_conv = _mlstm(
                (at("a_qk"), at("a_qk", qk_a), at("a_v"), at("a_o"), (p_gw, 0)),
                g["conv0"], g["c0"], g["n0"], g["m0"], conv_w[l], conv_b[l].reshape(1, -1),
                bias_if, hn_a[l].reshape(1, -1), o_a,
                row0=g["row0"], B=g["B"], T=g["T"], L=g["La"], NH=nh_a, DK=dk_a, DV=dv_a)
            o_b, b_st = _gla_b(
                (at("b_q"), at("b_k"), at("b_v"), at("b_g"), (p_gw, LANES)),
                g["sb0"], w_gk2p, b_gk[l].reshape(1, -1), hn_b[l].reshape(1, -1), o_b,
                row0=g["row0"], B=g["B"], T=g["T"], L=g["Lg"], NH=nh_b, DK=dk_b, DV=dv_b, HP=4)
            o_c, c_st = _gla_c(
                (at("c_q"), at("c_f"), at("c_i"), at("c_g")),
                g["sc0"], lb_logits, hn_c[l].reshape(1, -1), o_c,
                layer=l, row0=g["row0"], B=g["B"], T=g["T"], L=g["Lg"],
                NH=nh_c, DK=dk_c, DV=dv_c, HP=8)
            for lst, val in zip(outs, (a_c, a_n, a_m[:, 0, :nh_a], a_conv,
                                       jnp.swapaxes(b_st, -1, -2), jnp.swapaxes(c_st, -1, -2))):
                lst.append(val)

        merged = _merge(o_a, o_b, o_c, w_br_a16, w_br_b16, w_br_c16, l, *at("gates"))
        x = _out_proj(merged, w_out16, l, x, rsel, mod_l, 2)

        w_r = jnp.pad(jnp.concatenate([w_rg[l], w_re[l]], axis=1),
                      ((0, 0), (0, LANES - N_GROUPS - N_EXPERTS)))
        b_r = jnp.pad(jnp.concatenate([b_rg[l], b_re[l]]),
                      (0, LANES - N_GROUPS - N_EXPERTS)).reshape(1, LANES)
        h2, logits = _norm_mod_router(x, rsel, norm_moe[l], mod_l, 3, 4, w_r, b_r)
        eid, wts = _route(logits)
        rows, roww, blk_e, n_act, slot = _dispatch(eid, wts)
        yb = _expert_mlp(h2[rows], roww, blk_e, n_act, w_exp1, w_exp3, w_exp2, l)
        x = _gated_add(x, yb[slot[:, 0]], yb[slot[:, 1]], rsel, mod_l, 5)

    y_prompt = _final_norm(x, norm_final, 0, mp, 512).reshape(bp, tp, d)
    y_sample = _final_norm(x, norm_final, mp, ms, ms).reshape(bs, ts, d)
    stack = lambda lst: jnp.stack(lst)
    return (y_prompt, y_sample,
            stack(outs_p[0]), stack(outs_p[1]), stack(outs_p[2]), stack(outs_p[3]),
            stack(outs_p[4]), stack(outs_p[5]),
            stack(outs_s[0]), stack(outs_s[1]), stack(outs_s[2]), stack(outs_s[3]),
            stack(outs_s[4]), stack(outs_s[5]))
```

```python
import functools

import jax
import jax.numpy as jnp
import numpy as np
from jax import lax
from jax.experimental import pallas as pl
from jax.experimental.pallas import tpu as pltpu

F32 = jnp.float32
BF16 = jnp.bfloat16

NORM_EPS = 1e-6
NEG_BIG = -1e30
F_TINY = 1e-30
GLA_NORM = 16.0
CONV_W = 4
TOP_K = 2
N_GROUPS = 4
EXPERTS_PER_GROUP = 8
N_EXPERTS = N_GROUPS * EXPERTS_PER_GROUP
N_BRANCH = 3

LANES = 128
MOD_ROWS = 16
VMEM_LIMIT = 56 * 1024 * 1024

TM_BIG = 1040
TM_SMALL = 520
MOE_ROWS = 512
W_CHUNKS = 4
TN_PROJ = 1024
SUB = 16
SAFE_LOG = 60.0


def _cparams(sem):
    return pltpu.CompilerParams(dimension_semantics=sem, vmem_limit_bytes=VMEM_LIMIT)


def _dot(a, b):
    return jnp.dot(a, b, preferred_element_type=F32)


def _dot_nt(a, b):
    return lax.dot_general(a, b, (((1,), (1,)), ((), ())), preferred_element_type=F32)


def _dot_tn(a, b):
    return lax.dot_general(a, b, (((0,), (0,)), ((), ())), preferred_element_type=F32)


def _split3(x):
    hi = x.astype(BF16)
    r = x - hi.astype(F32)
    mid = r.astype(BF16)
    lo = (r - mid.astype(F32)).astype(BF16)
    return hi, mid, lo


def _dot_sel(sel, x):
    hi, mid, lo = _split3(x)
    return _dot(sel, hi) + _dot(sel, mid) + _dot(sel, lo)


def _dot_nt_sel(sel, x):
    hi, mid, lo = _split3(x)
    return _dot_nt(sel, hi) + _dot_nt(sel, mid) + _dot_nt(sel, lo)


def _dot_nt_f32(a, b):
    a_hi = a.astype(BF16)
    a_lo = (a - a_hi.astype(F32)).astype(BF16)
    b_hi = b.astype(BF16)
    b_lo = (b - b_hi.astype(F32)).astype(BF16)
    return _dot_nt(a_hi, b_hi) + _dot_nt(a_hi, b_lo) + _dot_nt(a_lo, b_hi)


def _r16(x):
    return x.astype(BF16).astype(F32)


def _pack_bf16_pairs(xb):
    n = xb.shape[1] // 2
    bits = pltpu.bitcast(xb.astype(F32), jnp.int32)
    return lax.shift_right_logical(bits[:, :n], 16) | bits[:, n:]


def _unpack_bf16_pairs(w):
    lo = pltpu.bitcast(lax.shift_left(w, 16), F32).astype(BF16)
    hi = pltpu.bitcast(w & jnp.int32(-65536), F32).astype(BF16)
    return lo, hi


def _sigmoid(x):
    return jax.nn.sigmoid(x)


def _silu(x):
    return x * jax.nn.sigmoid(x)


def _log_sigmoid(x):
    return jnp.minimum(x, 0.0) - jnp.log1p(jnp.exp(-jnp.abs(x)))


def _rms(x):
    return x * lax.rsqrt(jnp.mean(x * x, axis=-1, keepdims=True) + NORM_EPS)


def _tril(n):
    r = lax.broadcasted_iota(jnp.int32, (n, n), 0)
    c = lax.broadcasted_iota(jnp.int32, (n, n), 1)
    return jnp.where(r >= c, 1.0, 0.0).astype(BF16)


def _eye(n):
    r = lax.broadcasted_iota(jnp.int32, (n, n), 0)
    c = lax.broadcasted_iota(jnp.int32, (n, n), 1)
    return jnp.where(r == c, 1.0, 0.0).astype(BF16)


def _mod_kernel(c_ref, w_ref, b_ref, o_ref):
    c = c_ref[...]
    o_ref[0] = _dot(_silu(c).astype(BF16), w_ref[0].astype(BF16)) + b_ref[0]


def _modulation(c_all, w_ada, b_ada):
    depth, d, n = w_ada.shape
    tn = 1024
    return pl.pallas_call(
        _mod_kernel,
        grid=(depth, n // tn),
        in_specs=[pl.BlockSpec((MOD_ROWS, d), lambda l, j: (0, 0)),
                  pl.BlockSpec((1, d, tn), lambda l, j: (l, 0, j)),
                  pl.BlockSpec((1, 1, tn), lambda l, j: (l, 0, j))],
        out_specs=pl.BlockSpec((1, MOD_ROWS, tn), lambda l, j: (l, 0, j)),
        out_shape=jax.ShapeDtypeStruct((depth, MOD_ROWS, n), F32),
        compiler_params=_cparams(("arbitrary", "arbitrary")),
        name="modulation",
    )(c_all, w_ada, b_ada.reshape(depth, 1, n))


def _row_mod(mixed, r_ref, m_ref):
    if mixed:
        return _dot_sel(r_ref[...], m_ref[0])
    return m_ref[0, 0:1, :]


def _norm_mod_kernel(x_ref, r_ref, g_ref, sh_ref, sc_ref, o_ref):
    last = pl.num_programs(0) - 1

    def body(mixed):
        y = _rms(x_ref[...]) * g_ref[...]
        o_ref[...] = (y * (1.0 + _row_mod(mixed, r_ref, sc_ref)) + _row_mod(mixed, r_ref, sh_ref)
                      ).astype(o_ref.dtype)

    pl.when(pl.program_id(0) != last)(lambda: body(False))
    pl.when(pl.program_id(0) == last)(lambda: body(True))


def _norm_mod(x, rsel, gain, mod_l, sh_idx, sc_idx):
    m, d = x.shape
    tm = TM_SMALL
    return pl.pallas_call(
        _norm_mod_kernel,
        grid=(m // tm,),
        in_specs=[pl.BlockSpec((tm, d), lambda i: (i, 0)),
                  pl.BlockSpec((tm, MOD_ROWS), lambda i: (i, 0)),
                  pl.BlockSpec((1, d), lambda i: (0, 0)),
                  pl.BlockSpec((1, MOD_ROWS, d), lambda i: (0, 0, sh_idx)),
                  pl.BlockSpec((1, MOD_ROWS, d), lambda i: (0, 0, sc_idx))],
        out_specs=pl.BlockSpec((tm, d), lambda i: (i, 0)),
        out_shape=jax.ShapeDtypeStruct((m, d), BF16),
        compiler_params=_cparams(("arbitrary",)),
        name="norm_mod",
    )(x, rsel, gain.reshape(1, d), mod_l, mod_l)


def _norm_mod_router_kernel(x_ref, r_ref, g_ref, sh_ref, sc_ref, wr_ref, br_ref, o_ref, lg_ref):
    last = pl.num_programs(0) - 1

    def body(mixed):
        y = _rms(x_ref[...]) * g_ref[...]
        h = y * (1.0 + _row_mod(mixed, r_ref, sc_ref)) + _row_mod(mixed, r_ref, sh_ref)
        hb = h.astype(BF16)
        o_ref[...] = _pack_bf16_pairs(hb)
        lg_ref[...] = _dot(hb, wr_ref[...].astype(BF16)) + br_ref[...]

    pl.when(pl.program_id(0) != last)(lambda: body(False))
    pl.when(pl.program_id(0) == last)(lambda: body(True))


def _norm_mod_router(x, rsel, gain, mod_l, sh_idx, sc_idx, w_r, b_r):
    m, d = x.shape
    tm = TM_SMALL
    return pl.pallas_call(
        _norm_mod_router_kernel,
        grid=(m // tm,),
        in_specs=[pl.BlockSpec((tm, d), lambda i: (i, 0)),
                  pl.BlockSpec((tm, MOD_ROWS), lambda i: (i, 0)),
                  pl.BlockSpec((1, d), lambda i: (0, 0)),
                  pl.BlockSpec((1, MOD_ROWS, d), lambda i: (0, 0, sh_idx)),
                  pl.BlockSpec((1, MOD_ROWS, d), lambda i: (0, 0, sc_idx)),
                  pl.BlockSpec((d, LANES), lambda i: (0, 0)),
                  pl.BlockSpec((1, LANES), lambda i: (0, 0))],
        out_specs=[pl.BlockSpec((tm, d // 2), lambda i: (i, 0)),
                   pl.BlockSpec((tm, LANES), lambda i: (i, 0))],
        out_shape=[jax.ShapeDtypeStruct((m, d // 2), jnp.int32),
                   jax.ShapeDtypeStruct((m, LANES), F32)],
        compiler_params=_cparams(("arbitrary",)),
        name="norm_mod_router",
    )(x, rsel, gain.reshape(1, d), mod_l, mod_l, w_r, b_r)


def _final_norm_kernel(x_ref, g_ref, o_ref):
    o_ref[...] = _rms(x_ref[...]) * g_ref[...]


def _final_norm(x, gain, row0, rows, tm):
    d = x.shape[1]
    blk0 = row0 // tm
    return pl.pallas_call(
        _final_norm_kernel,
        grid=(rows // tm,),
        in_specs=[pl.BlockSpec((tm, d), lambda i: (blk0 + i, 0)),
                  pl.BlockSpec((1, d), lambda i: (0, 0))],
        out_specs=pl.BlockSpec((tm, d), lambda i: (i, 0)),
        out_shape=jax.ShapeDtypeStruct((rows, d), F32),
        compiler_params=_cparams(("arbitrary",)),
        name="final_norm",
    )(x, gain.reshape(1, d))


def _matmul_kernel(a_ref, b_ref, o_ref):
    o_ref[...] = _dot(a_ref[...], b_ref[...]).astype(o_ref.dtype)


def _matmul(a, b, out_dtype, tm, tn):
    m, k = a.shape
    n = b.shape[1]
    return pl.pallas_call(
        _matmul_kernel,
        grid=(m // tm, n // tn),
        in_specs=[pl.BlockSpec((tm, k), lambda i, j: (i, 0)),
                  pl.BlockSpec((k, tn), lambda i, j: (0, j))],
        out_specs=pl.BlockSpec((tm, tn), lambda i, j: (i, j)),
        out_shape=jax.ShapeDtypeStruct((m, n), out_dtype),
        compiler_params=_cparams(("arbitrary", "arbitrary")),
        name="in_proj",
    )(a, b)


def _merge_kernel(oa_ref, ob_ref, oc_ref, wa_ref, wb_ref, wc_ref, ga_ref, gb_ref, gc_ref, o_ref):
    acc = _sigmoid(ga_ref[...]) * _dot(oa_ref[...], wa_ref[...])
    acc = acc + _sigmoid(gb_ref[...]) * _dot(ob_ref[...], wb_ref[...])
    acc = acc + _sigmoid(gc_ref[...]) * _dot(oc_ref[...], wc_ref[...])
    o_ref[...] = acc.astype(o_ref.dtype)


def _merge(o_a, o_b, o_c, w_a, w_b, w_c, layer, p, gate_col0):
    m, kw = o_a.shape
    d = w_a.shape[2]
    tm, tn = TM_SMALL, 512
    g0 = gate_col0 // tn
    gstride = d // tn
    o_spec = pl.BlockSpec((tm, kw), lambda j, i: (i, 0))
    w_spec = pl.BlockSpec((None, kw, tn), lambda j, i: (layer, 0, j))

    def g_spec(br):
        return pl.BlockSpec((tm, tn), lambda j, i: (i, g0 + br * gstride + j))

    return pl.pallas_call(
        _merge_kernel,
        grid=(d // tn, m // tm),
        in_specs=[o_spec, o_spec, o_spec, w_spec, w_spec, w_spec, g_spec(0), g_spec(1), g_spec(2)],
        out_specs=pl.BlockSpec((tm, tn), lambda j, i: (i, j)),
        out_shape=jax.ShapeDtypeStruct((m, d), BF16),
        compiler_params=_cparams(("arbitrary", "arbitrary")),
        name="merge",
    )(o_a, o_b, o_c, w_a, w_b, w_c, p, p, p)


def _out_proj_kernel(a_ref, w_ref, x_ref, r_ref, g_ref, o_ref):
    last = pl.num_programs(0) - 1
    acc = _dot(a_ref[...], w_ref[...])

    def body(mixed):
        o_ref[...] = x_ref[...] + _row_mod(mixed, r_ref, g_ref) * acc

    pl.when(pl.program_id(0) != last)(lambda: body(False))
    pl.when(pl.program_id(0) == last)(lambda: body(True))


def _out_proj(a, w, layer, x, rsel, mod_l, g_idx):
    m, k = a.shape
    d = w.shape[2]
    tm, tn = TM_BIG, 512
    nj = d // tn
    return pl.pallas_call(
        _out_proj_kernel,
        grid=(m // tm, nj),
        in_specs=[pl.BlockSpec((tm, k), lambda i, j: (i, 0)),
                  pl.BlockSpec((None, k, tn), lambda i, j: (layer, 0, j)),
                  pl.BlockSpec((tm, tn), lambda i, j: (i, j)),
                  pl.BlockSpec((tm, MOD_ROWS), lambda i, j: (i, 0)),
                  pl.BlockSpec((1, MOD_ROWS, tn), lambda i, j: (0, 0, g_idx * nj + j))],
        out_specs=pl.BlockSpec((tm, tn), lambda i, j: (i, j)),
        out_shape=jax.ShapeDtypeStruct((m, d), F32),
        compiler_params=_cparams(("arbitrary", "arbitrary")),
        name="out_proj",
    )(a, w, x, rsel, mod_l)


def _gated_add_kernel(x_ref, y0_ref, y1_ref, r_ref, g_ref, o_ref):
    last = pl.num_programs(0) - 1

    def body(mixed):
        o_ref[...] = x_ref[...] + _row_mod(mixed, r_ref, g_ref) * (y0_ref[...] + y1_ref[...])

    pl.when(pl.program_id(0) != last)(lambda: body(False))
    pl.when(pl.program_id(0) == last)(lambda: body(True))


def _gated_add(x, y, rsel, mod_l, g_idx):
    m, d = x.shape
    tm, tn = TM_SMALL, 1024
    nj = d // tn
    spec = pl.BlockSpec((tm, tn), lambda i, j: (i, j))
    return pl.pallas_call(
        _gated_add_kernel,
        grid=(m // tm, nj),
        in_specs=[spec, spec, pl.BlockSpec((tm, tn), lambda i, j: (i, nj + j)),
                  pl.BlockSpec((tm, MOD_ROWS), lambda i, j: (i, 0)),
                  pl.BlockSpec((1, MOD_ROWS, tn), lambda i, j: (0, 0, g_idx * nj + j))],
        out_specs=spec,
        out_shape=jax.ShapeDtypeStruct((m, d), F32),
        compiler_params=_cparams(("arbitrary", "arbitrary")),
        name="moe_combine",
    )(x, y, y, rsel, mod_l)


def _alias_prev(kern, in_specs, args, o_prev):
    if o_prev is None:
        return kern, in_specs, args, {}
    n = len(args)

    def wrapped(*refs):
        return kern(*refs[:n], *refs[n + 1:])

    return wrapped, in_specs + [pl.BlockSpec(memory_space=pl.ANY)], args + [o_prev], {n: 0}


def _mlstm_kernel(qp_ref, kp_ref, v_ref, og_ref, if_ref, conv0_ref, c0_ref, n0_ref, m0_ref,
                  cw_ref, cb_ref, bif_ref, hn_ref,
                  o_ref, cout_ref, nout_ref, mout_ref, convout_ref,
                  c_s, n_s, m_s, ubuf, q_s, k_s, *, L, NH, DK, DV):
    ci = pl.program_id(1)
    QK = NH * DK

    @pl.when(ci == 0)
    def _():
        c_s[...] = c0_ref[0]
        n_s[...] = n0_ref[0]
        m_s[...] = m0_ref[0]
        ubuf[8 - (CONV_W - 1):8, :] = conv0_ref[0]

    ubuf[8:8 + L, 0:QK] = qp_ref[...]
    ubuf[8:8 + L, QK:2 * QK] = kp_ref[...]
    y = cb_ref[...]
    for j in range(CONV_W):
        y = y + ubuf[8 - (CONV_W - 1) + j:8 - (CONV_W - 1) + j + L, :] * cw_ref[j:j + 1, :]
    ubuf[0:8, :] = ubuf[L:L + 8, :]
    qk = _silu(y)
    q_s[...] = qk[:, 0:QK] * (DK ** -0.5)
    k_s[...] = qk[:, QK:2 * QK]

    ifv = if_ref[...] + bif_ref[...]
    b_all = _dot_sel(_tril(L), _log_sigmoid(ifv))
    eye = _eye(LANES)
    ig_t = _dot_nt_sel(eye, ifv)
    b_t = _dot_nt_sel(eye, b_all)
    row = lax.broadcasted_iota(jnp.int32, (L, L), 0)
    col = lax.broadcasted_iota(jnp.int32, (L, L), 1)
    causal = row >= col
    lane = lax.broadcasted_iota(jnp.int32, (1, LANES), 1)
    m_old = m_s[...]
    m_new = m_old

    for h in range(NH):
        b_c = b_all[:, NH + h:NH + h + 1]
        ig_c = ifv[:, h:h + 1]
        b_r = b_t[NH + h:NH + h + 1, :]
        ig_r = ig_t[h:h + 1, :]
        m_prev = m_old[:, h:h + 1]
        log_d = jnp.where(causal, b_c - b_r + ig_r, NEG_BIG)
        inter = b_c + m_prev
        mt = jnp.maximum(jnp.max(log_d, axis=-1, keepdims=True), inter)
        dm = jnp.exp(log_d - mt)
        sc = jnp.exp(inter - mt)
        qh = q_s[:, h * DK:(h + 1) * DK]
        kh = k_s[:, h * DK:(h + 1) * DK]
        vh = v_ref[:, h * DV:(h + 1) * DV]
        qb, kb, vb = qh.astype(BF16), kh.astype(BF16), vh.astype(BF16)
        c_h = c_s[h]
        n_h = n_s[h:h + 1, :]
        s = _dot_nt(qb, kb) * dm
        num = sc * _dot_nt(qb, c_h.astype(BF16)) + _dot(s.astype(BF16), vb)
        den = (sc * jnp.sum(qb.astype(F32) * _r16(n_h), axis=-1, keepdims=True)
               + jnp.sum(s, axis=-1, keepdims=True))
        hc = num / jnp.maximum(jnp.abs(den), jnp.exp(-mt))
        mt_l = mt[L - 1:L, :]
        w_l = jnp.exp((b_c[L - 1:L, :] - b_c) + ig_c - mt_l)
        s_l = sc[L - 1:L, :]
        c_s[h] = s_l * c_h + _dot_tn(vb, (kh * w_l).astype(BF16))
        n_s[h:h + 1, :] = s_l * n_h + jnp.sum(_r16(w_l) * kb.astype(F32), axis=0, keepdims=True)
        m_new = jnp.where(lane == h, mt_l, m_new)
        out = _rms(hc) * hn_ref[:, h * DV:(h + 1) * DV] * _sigmoid(og_ref[:, h * DV:(h + 1) * DV])
        o_ref[:, h * DV:(h + 1) * DV] = out.astype(o_ref.dtype)

    m_s[...] = m_new

    @pl.when(ci == pl.num_programs(1) - 1)
    def _():
        cout_ref[0] = c_s[...]
        nout_ref[0] = n_s[...]
        mout_ref[0] = m_s[...]
        convout_ref[0] = ubuf[8 - (CONV_W - 1):8, :]


def _mlstm(srcs, conv0, c0, n0, m0, conv_w, conv_b, bias_if, hn, o_prev,
           *, row0, B, T, L, NH, DK, DV):
    QK, W = NH * DK, NH * DV
    nC = T // L
    rb0 = row0 // L
    (a_q, c_q), (a_k, c_k), (a_v, c_v), (a_o, c_o), (a_if, c_if) = srcs

    def rows(b, c):
        return rb0 + b * nC + c

    kern = functools.partial(_mlstm_kernel, L=L, NH=NH, DK=DK, DV=DV)
    in_specs = [pl.BlockSpec((L, QK), lambda b, c: (rows(b, c), c_q // QK)),
                  pl.BlockSpec((L, QK), lambda b, c: (rows(b, c), c_k // QK)),
                  pl.BlockSpec((L, W), lambda b, c: (rows(b, c), c_v // W)),
                  pl.BlockSpec((L, W), lambda b, c: (rows(b, c), c_o // W)),
                  pl.BlockSpec((L, LANES), lambda b, c: (rows(b, c), c_if // LANES)),
                  pl.BlockSpec((1, CONV_W - 1, 2 * QK), lambda b, c: (b, 0, 0)),
                  pl.BlockSpec((1, NH, DV, DK), lambda b, c: (b, 0, 0, 0)),
                  pl.BlockSpec((1, NH, DK), lambda b, c: (b, 0, 0)),
                  pl.BlockSpec((1, 1, LANES), lambda b, c: (b, 0, 0)),
                  pl.BlockSpec((CONV_W, 2 * QK), lambda b, c: (0, 0)),
                  pl.BlockSpec((1, 2 * QK), lambda b, c: (0, 0)),
                  pl.BlockSpec((1, LANES), lambda b, c: (0, 0)),
                  pl.BlockSpec((1, W), lambda b, c: (0, 0))]
    args = [a_q, a_k, a_v, a_o, a_if, conv0, c0, n0, m0, conv_w, conv_b, bias_if, hn]
    kern, in_specs, args, aliases = _alias_prev(kern, in_specs, args, o_prev)
    return pl.pallas_call(
        kern,
        grid=(B, nC),
        in_specs=in_specs,
        out_specs=[pl.BlockSpec((L, W), lambda b, c: (rows(b, c), 0)),
                   pl.BlockSpec((1, NH, DV, DK), lambda b, c: (b, 0, 0, 0)),
                   pl.BlockSpec((1, NH, DK), lambda b, c: (b, 0, 0)),
                   pl.BlockSpec((1, 1, LANES), lambda b, c: (b, 0, 0)),
                   pl.BlockSpec((1, CONV_W - 1, 2 * QK), lambda b, c: (b, 0, 0))],
        out_shape=[jax.ShapeDtypeStruct((a_q.shape[0], W), BF16),
                   jax.ShapeDtypeStruct((B, NH, DV, DK), F32),
                   jax.ShapeDtypeStruct((B, NH, DK), F32),
                   jax.ShapeDtypeStruct((B, 1, LANES), F32),
                   jax.ShapeDtypeStruct((B, CONV_W - 1, 2 * QK), F32)],
        scratch_shapes=[pltpu.VMEM((NH, DV, DK), F32),
                        pltpu.VMEM((NH, DK), F32),
                        pltpu.VMEM((1, LANES), F32),
                        pltpu.VMEM((L + 8, 2 * QK), F32),
                        pltpu.VMEM((L, QK), F32),
                        pltpu.VMEM((L, QK), F32)],
        input_output_aliases=aliases,
        compiler_params=_cparams(("arbitrary", "arbitrary")),
        name="mixer_mlstm",
    )(*args)


def _gla_intra_exact(q, k, b, L, c):
    dk = q.shape[1]
    nsub = L // c
    q3 = q.reshape(nsub, c, dk)
    k3 = k.reshape(nsub, c, dk)
    b3 = b.reshape(nsub, c, dk)
    t_idx = lax.broadcasted_iota(jnp.int32, (1, c, 1), 1)
    s_idx = lax.broadcasted_iota(jnp.int32, (1, 1, c), 2)
    a_diag = jnp.zeros((nsub, c, c), F32)
    for s in range(c):
        arg = jnp.where(t_idx >= s, b3 - b3[:, s:s + 1, :], NEG_BIG)
        col_s = jnp.sum(q3 * k3[:, s:s + 1, :] * jnp.exp(arg), axis=-1, keepdims=True)
        a_diag = jnp.where(s_idx == s, col_s, a_diag)
    a_diag = a_diag.reshape(L, c)
    if nsub == 1:
        return a_diag
    row = lax.broadcasted_iota(jnp.int32, (L, L), 0)
    col = lax.broadcasted_iota(jnp.int32, (L, L), 1)
    rep_r = lax.broadcasted_iota(jnp.int32, (c, L), 0)
    rep_c = lax.broadcasted_iota(jnp.int32, (c, L), 1)
    rep = jnp.where((rep_c & (c - 1)) == rep_r, 1.0, 0.0).astype(BF16)
    a = jnp.where((row & -c) == (col & -c), _dot(a_diag.astype(BF16), rep), 0.0)
    blocks = [jnp.zeros((c, L), F32)]
    for i in range(1, nsub):
        r_i = b[i * c - 1:i * c, :]
        q_i = q[i * c:(i + 1) * c, :] * jnp.exp(b[i * c:(i + 1) * c, :] - r_i)
        k_i = k * jnp.exp(jnp.minimum(r_i - b, 0.0))
        blocks.append(_dot_nt_f32(q_i, k_i))
    return a + jnp.where((col & -c) < (row & -c), jnp.concatenate(blocks, axis=0), 0.0)


def _rows_of(vecs, c):
    n = vecs[0].shape[1]
    parts = [jnp.broadcast_to(v, (c, n)) for v in vecs]
    return parts[0] if len(parts) == 1 else jnp.concatenate(parts, axis=0)


def _gla_chunk_fact(q, k, v, b, st, L):
    c = min(SUB, L)
    nsub = L // c
    dk = q.shape[1]
    r = ([jnp.zeros((1, dk), F32)] + [b[j * c - 1:j * c, :] for j in range(1, nsub)]
         + [b[L - 1:L, :]])
    q_t = q * jnp.exp(b - _rows_of(r[:nsub], c))
    k_h = k * jnp.exp(_rows_of(r[1:], c) - b)
    o = _dot_nt((q_t * _rows_of([jnp.exp(rj) for rj in r[:nsub]], c)).astype(BF16),
                st.astype(BF16))
    row = lax.broadcasted_iota(jnp.int32, (L, L), 0)
    col = lax.broadcasted_iota(jnp.int32, (L, L), 1)
    blocks = []
    for i in range(nsub):
        scale = _rows_of([jnp.exp(r[i] - r[j + 1]) if j <= i else jnp.ones((1, dk), F32)
                          for j in range(nsub)], c)
        blocks.append(_dot_nt_f32(q_t[i * c:(i + 1) * c, :], k_h * scale))
    a = blocks[0] if nsub == 1 else jnp.concatenate(blocks, axis=0)
    a = jnp.where(col <= row, a, 0.0)
    vb = v.astype(BF16)
    o = o + _dot(a.astype(BF16), vb)
    k_l = k_h * _rows_of([jnp.exp(r[nsub] - rj) for rj in r[1:]], c)
    st_new = st * jnp.exp(r[nsub]) + _dot_tn(vb, k_l.astype(BF16))
    return o, st_new


def _gla_cum_decay(g, L):
    c = min(SUB, L)
    nsub = L // c
    b = _dot_sel(_tril(L), g)
    drops = [b[c - 1:c, :]] + [b[(i + 1) * c - 1:(i + 1) * c, :] - b[i * c - 1:i * c, :]
                               for i in range(1, nsub)]
    return b, jnp.min(drops[0] if nsub == 1 else jnp.concatenate(drops, axis=0))


def _either(pred, body):
    pl.when(pred)(lambda: body(True))
    pl.when(jnp.logical_not(pred))(lambda: body(False))


def _gla_chunk(q, k, v, b, st, L, fact):
    if fact:
        return _gla_chunk_fact(q, k, v, b, st, L)
    c = min(SUB, L)
    b_l = b[L - 1:L, :]
    o = _dot_nt((q * jnp.exp(b)).astype(BF16), st.astype(BF16))
    a = _gla_intra_exact(q, k, b, L, c)
    vb = v.astype(BF16)
    o = o + _dot(a.astype(BF16), vb)
    st_new = st * jnp.exp(b_l) + _dot_tn(vb, (k * jnp.exp(b_l - b)).astype(BF16))
    return o, st_new


def _gla_b_kernel(q_ref, k_ref, v_ref, gt_ref, lr_ref, s0_ref, w2_ref, bgk_ref, hn_ref,
                  o_ref, sout_ref, st_s, *, L, HP, DK, DV):
    ci = pl.program_id(2)

    @pl.when(ci == 0)
    def _():
        st_s[...] = s0_ref[0]

    z = _dot(lr_ref[...].astype(BF16), w2_ref[...].astype(BF16)) + bgk_ref[...]
    b, worst = _gla_cum_decay(_log_sigmoid(z) / GLA_NORM, L)

    def body(fact):
        for h in range(HP):
            sl = slice(h * DK, (h + 1) * DK)
            sv = slice(h * DV, (h + 1) * DV)
            o, st_new = _gla_chunk(q_ref[:, sl] * (DK ** -0.5), k_ref[:, sl], v_ref[:, sv],
                                   b[:, sl], st_s[h], L, fact)
            st_s[h] = st_new
            o_ref[:, sv] = (_rms(o) * hn_ref[:, sv] * _silu(gt_ref[:, sv])).astype(o_ref.dtype)

    _either(worst > -SAFE_LOG, body)

    @pl.when(ci == pl.num_programs(2) - 1)
    def _():
        sout_ref[0] = st_s[...]


def _gla_b(srcs, s0t, w_gk2p, b_gk, hn, o_prev, *, row0, B, T, L, NH, DK, DV, HP):
    nC = T // L
    rb0 = row0 // L
    (a_q, c_q), (a_k, c_k), (a_v, c_v), (a_g, c_g), (a_lr, c_lr) = srcs
    wq, wv = HP * DK, HP * DV

    def rows(b, c):
        return rb0 + b * nC + c

    kern = functools.partial(_gla_b_kernel, L=L, HP=HP, DK=DK, DV=DV)
    in_specs = [pl.BlockSpec((L, wq), lambda b, g, c: (rows(b, c), c_q // wq + g)),
                pl.BlockSpec((L, wq), lambda b, g, c: (rows(b, c), c_k // wq + g)),
                pl.BlockSpec((L, wv), lambda b, g, c: (rows(b, c), c_v // wv + g)),
                pl.BlockSpec((L, wv), lambda b, g, c: (rows(b, c), c_g // wv + g)),
                pl.BlockSpec((L, LANES), lambda b, g, c: (rows(b, c), c_lr // LANES)),
                pl.BlockSpec((1, HP, DV, DK), lambda b, g, c: (b, g, 0, 0)),
                pl.BlockSpec((LANES, wq), lambda b, g, c: (0, g)),
                pl.BlockSpec((1, wq), lambda b, g, c: (0, g)),
                pl.BlockSpec((1, wv), lambda b, g, c: (0, g))]
    args = [a_q, a_k, a_v, a_g, a_lr, s0t, w_gk2p, b_gk, hn]
    kern, in_specs, args, aliases = _alias_prev(kern, in_specs, args, o_prev)
    return pl.pallas_call(
        kern,
        grid=(B, NH // HP, nC),
        in_specs=in_specs,
        out_specs=[pl.BlockSpec((L, wv), lambda b, g, c: (rows(b, c), g)),
                   pl.BlockSpec((1, HP, DV, DK), lambda b, g, c: (b, g, 0, 0))],
        out_shape=[jax.ShapeDtypeStruct((a_q.shape[0], NH * DV), BF16),
                   jax.ShapeDtypeStruct((B, NH, DV, DK), F32)],
        scratch_shapes=[pltpu.VMEM((HP, DV, DK), F32)],
        input_output_aliases=aliases,
        compiler_params=_cparams(("arbitrary", "arbitrary", "arbitrary")),
        name="mixer_gla",
    )(*args)


def _gla_c_kernel(q_ref, f_ref, i_ref, gt_ref, s0_ref, lbl_ref, hn_ref,
                  o_ref, sout_ref, st_s, *, L, HP, DK, DV, layer):
    ci = pl.program_id(2)

    @pl.when(ci == 0)
    def _():
        st_s[...] = s0_ref[0]

    lbl = lbl_ref[...]
    e = jnp.exp(lbl - jnp.max(lbl, axis=0, keepdims=True))
    sm = e / jnp.sum(e, axis=0, keepdims=True)
    lb = jnp.sum(sm[0:layer + 1, :], axis=0, keepdims=True) - sm[0:1, :]

    f = lb + (1.0 - lb) * _sigmoid(f_ref[...])
    b, worst = _gla_cum_decay(jnp.log(jnp.maximum(f, F_TINY)), L)

    def body(fact):
        for h in range(HP):
            sl = slice(h * DK, (h + 1) * DK)
            sv = slice(h * DV, (h + 1) * DV)
            o, st_new = _gla_chunk(_silu(q_ref[:, sl]), 1.0 - f[:, sl], i_ref[:, sv], b[:, sl],
                                   st_s[h], L, fact)
            st_s[h] = st_new
            o_ref[:, sv] = (_rms(o) * hn_ref[:, sv] * _silu(gt_ref[:, sv])).astype(o_ref.dtype)

    _either(worst > -SAFE_LOG, body)

    @pl.when(ci == pl.num_programs(2) - 1)
    def _():
        sout_ref[0] = st_s[...]


def _gla_c(srcs, s0t, lb_logits, hn, o_prev, *, layer, row0, B, T, L, NH, DK, DV, HP):
    nC = T // L
    rb0 = row0 // L
    (a_q, c_q), (a_f, c_f), (a_i, c_i), (a_g, c_g) = srcs
    wq, wv = HP * DK, HP * DV

    def rows(b, c):
        return rb0 + b * nC + c

    kern = functools.partial(_gla_c_kernel, L=L, HP=HP, DK=DK, DV=DV, layer=layer)
    in_specs = [pl.BlockSpec((L, wq), lambda b, g, c: (rows(b, c), c_q // wq + g)),
                pl.BlockSpec((L, wq), lambda b, g, c: (rows(b, c), c_f // wq + g)),
                pl.BlockSpec((L, wv), lambda b, g, c: (rows(b, c), c_i // wv + g)),
                pl.BlockSpec((L, wv), lambda b, g, c: (rows(b, c), c_g // wv + g)),
                pl.BlockSpec((1, HP, DV, DK), lambda b, g, c: (b, g, 0, 0)),
                pl.BlockSpec((lb_logits.shape[0], wq), lambda b, g, c: (0, g)),
                pl.BlockSpec((1, wv), lambda b, g, c: (0, g))]
    args = [a_q, a_f, a_i, a_g, s0t, lb_logits, hn]
    kern, in_specs, args, aliases = _alias_prev(kern, in_specs, args, o_prev)
    return pl.pallas_call(
        kern,
        grid=(B, NH // HP, nC),
        in_specs=in_specs,
        out_specs=[pl.BlockSpec((L, wv), lambda b, g, c: (rows(b, c), g)),
                   pl.BlockSpec((1, HP, DV, DK), lambda b, g, c: (b, g, 0, 0))],
        out_shape=[jax.ShapeDtypeStruct((a_q.shape[0], NH * DV), BF16),
                   jax.ShapeDtypeStruct((B, NH, DV, DK), F32)],
        scratch_shapes=[pltpu.VMEM((HP, DV, DK), F32)],
        input_output_aliases=aliases,
        compiler_params=_cparams(("arbitrary", "arbitrary", "arbitrary")),
        name="mixer_hgrn2",
    )(*args)


def _new_expert(be_ref):
    b = pl.program_id(1)
    return (b == 0) | (be_ref[b] != be_ref[jnp.maximum(b - 1, 0)])


def _round_chunks(chunk_refs, w_s):
    rc = chunk_refs[0].shape[0]
    for c, ref in enumerate(chunk_refs):
        w_s[c * rc:(c + 1) * rc, :] = ref[...].astype(BF16)


def _gmm1_kernel(be_ref, na_ref, x_ref, *refs):
    w1_refs, w3_refs = refs[:W_CHUNKS], refs[W_CHUNKS:2 * W_CHUNKS]
    o_ref, w1_s, w3_s = refs[2 * W_CHUNKS:]

    @pl.when(_new_expert(be_ref))
    def _():
        _round_chunks(w1_refs, w1_s)
        _round_chunks(w3_refs, w3_s)

    @pl.when(pl.program_id(1) < na_ref[0])
    def _():
        x_lo, x_hi = _unpack_bf16_pairs(x_ref[...])
        half = x_lo.shape[1]
        up = _dot(x_lo, w1_s[0:half, :]) + _dot(x_hi, w1_s[half:, :])
        gate = _dot(x_lo, w3_s[0:half, :]) + _dot(x_hi, w3_s[half:, :])
        o_ref[...] = (_silu(up) * gate).astype(o_ref.dtype)

    @pl.when(pl.program_id(1) >= na_ref[0])
    def _():
        o_ref[...] = jnp.zeros_like(o_ref)


def _gmm2_kernel(be_ref, na_ref, h_ref, *refs):
    w2_refs = refs[:W_CHUNKS]
    rw_ref, o_ref, w2_s = refs[W_CHUNKS:]

    @pl.when(_new_expert(be_ref))
    def _():
        _round_chunks(w2_refs, w2_s)

    @pl.when(pl.program_id(1) < na_ref[0])
    def _():
        o_ref[...] = _dot(h_ref[...], w2_s[...]) * rw_ref[...]

    @pl.when(pl.program_id(1) >= na_ref[0])
    def _():
        o_ref[...] = jnp.zeros_like(o_ref)


def _expert_mlp(xg, roww, blk_e, n_act, w1, w3, w2, layer):
    rows, dh = xg.shape
    d, f = w1.shape[2], w1.shape[3]
    r = MOE_ROWS
    nb = rows // r
    tf, td = f // 2, d // 2

    def act(b, na):
        return jnp.minimum(b, na[0] - 1)

    def chunks(nrows, ncols):
        rc = nrows // W_CHUNKS
        return [pl.BlockSpec((None, None, rc, ncols),
                             lambda j, b, be, na, c=c: (layer, be[b], c, j))
                for c in range(W_CHUNKS)]

    hb = pl.pallas_call(
        _gmm1_kernel,
        grid_spec=pltpu.PrefetchScalarGridSpec(
            num_scalar_prefetch=2, grid=(f // tf, nb),
            in_specs=[pl.BlockSpec((r, dh), lambda j, b, be, na: (act(b, na), 0))]
            + chunks(d, tf) + chunks(d, tf),
            out_specs=pl.BlockSpec((r, tf), lambda j, b, be, na: (b, j)),
            scratch_shapes=[pltpu.VMEM((d, tf), BF16), pltpu.VMEM((d, tf), BF16)]),
        out_shape=jax.ShapeDtypeStruct((rows, f), BF16),
        compiler_params=_cparams(("arbitrary", "arbitrary")),
        name="expert_up",
    )(blk_e, n_act, xg, *([w1] * W_CHUNKS), *([w3] * W_CHUNKS))
    return pl.pallas_call(
        _gmm2_kernel,
        grid_spec=pltpu.PrefetchScalarGridSpec(
            num_scalar_prefetch=2, grid=(d // td, nb),
            in_specs=[pl.BlockSpec((r, f), lambda j, b, be, na: (act(b, na), 0))]
            + chunks(f, td)
            + [pl.BlockSpec((r, 1), lambda j, b, be, na: (act(b, na), 0))],
            out_specs=pl.BlockSpec((r, td), lambda j, b, be, na: (b, j)),
            scratch_shapes=[pltpu.VMEM((f, td), BF16)]),
        out_shape=jax.ShapeDtypeStruct((rows, d), F32),
        compiler_params=_cparams(("arbitrary", "arbitrary")),
        name="expert_down",
    )(blk_e, n_act, hb, *([w2] * W_CHUNKS), roww)


def _route(logits):
    n = logits.shape[0]
    pg = jax.nn.softmax(logits[:, :N_GROUPS], axis=-1)
    grp = jnp.argmax(pg, axis=-1)
    p_grp = jnp.max(pg, axis=-1)
    le = logits[:, N_GROUPS:N_GROUPS + N_EXPERTS].reshape(n, N_GROUPS, EXPERTS_PER_GROUP)
    le_g = le[jnp.arange(n), grp]
    top_logit, top_j = lax.top_k(le_g, TOP_K)
    wts = jax.nn.softmax(top_logit, axis=-1) * p_grp[:, None]
    eid = (grp[:, None] * EXPERTS_PER_GROUP + top_j).astype(jnp.int32)
    return eid, wts


def _dispatch(eid, wts):
    n = eid.shape[0]
    a = n * TOP_K
    r = MOE_ROWS
    nb = -(-a // r) + N_EXPERTS
    flat_e = eid.reshape(a)
    onehot = (flat_e[:, None] == jnp.arange(N_EXPERTS, dtype=jnp.int32)[None, :]).astype(jnp.int32)
    seen = jnp.cumsum(onehot, axis=0)
    counts = seen[-1]
    padded = (counts + r - 1) // r * r
    pend = jnp.cumsum(padded)
    pstart = pend - padded
    dest = jnp.sum(onehot * (seen - 1 + pstart[None, :]), axis=1).astype(jnp.int32)
    rows = jnp.zeros((nb * r,), jnp.int32).at[dest].set(jnp.arange(a, dtype=jnp.int32) // TOP_K)
    roww = jnp.zeros((nb * r,), F32).at[dest].set(wts.reshape(a))
    n_act = (pend[-1] // r).astype(jnp.int32)
    blk = jnp.arange(nb, dtype=jnp.int32)
    blk_e = jnp.minimum(jnp.searchsorted(pend, jnp.minimum(blk, n_act - 1) * r, side='right'),
                        N_EXPERTS - 1).astype(jnp.int32)
    return rows, roww.reshape(nb * r, 1), blk_e, n_act.reshape(1), dest.reshape(n, TOP_K)


def kernel(x_prompt, x_sample, c_prompt, c_sample, state_a_C, state_a_n, state_a_m, state_a_conv, state_b_S, state_c_S, w_ada, b_ada, norm_mix, w_in, conv_w, conv_b, b_gate_a, hn_a, w_gk2, b_gk, hn_b, lb_logits, hn_c, w_br_a, w_br_b, w_br_c, w_out, norm_moe, w_rg, b_rg, w_re, b_re, w_exp1, w_exp3, w_exp2, norm_final):
    depth = w_ada.shape[0]
    bp, tp, d = x_prompt.shape
    bs, ts, _ = x_sample.shape
    nh_a, dv_a, dk_a = state_a_C.shape[2:]
    nh_b, dk_b, dv_b = state_b_S.shape[2:]
    nh_c, dk_c, dv_c = state_c_S.shape[2:]
    gate_rank = w_gk2.shape[1]
    qk_a, w_a = nh_a * dk_a, nh_a * dv_a
    qk_b, w_b = nh_b * dk_b, nh_b * dv_b
    qk_c, w_c = nh_c * dk_c, nh_c * dv_c
    mp, ms = bp * tp, bs * ts
    m = mp + ms
    assert bp == 1 and 1 + bs <= MOD_ROWS and m % TM_BIG == 0 and ms <= TM_SMALL

    sizes = (2 * qk_a, w_a, 2 * nh_a, w_a, qk_b, qk_b, w_b, gate_rank, w_b,
             qk_c, qk_c, w_c, w_c, N_BRANCH * d)
    names = ("a_qk", "a_v", "a_if", "a_o", "b_q", "b_k", "b_v", "b_lr", "b_g",
             "c_q", "c_f", "c_i", "c_g", "gates")
    src, o = {}, 0
    for nm, sz in zip(names, sizes):
        src[nm] = o
        o += sz
    n_src = o
    lr_lane = src["b_lr"] % LANES
    assert src["a_if"] % LANES == 0 and lr_lane + gate_rank <= LANES
    ranges = ((src["a_qk"], src["a_if"], ("a_qk", "a_v")),
              (src["a_o"], src["b_lr"], ("a_o", "b_q", "b_k", "b_v")),
              (src["b_g"], n_src, ("b_g", "c_q", "c_f", "c_i", "c_g", "gates")))
    where = {}
    for ri, (s0, s1, members) in enumerate(ranges):
        for nm in members:
            where[nm] = (ri, src[nm] - s0)
    gate_windows = (src["a_if"], src["b_lr"] - lr_lane)

    x = jnp.concatenate([x_prompt.reshape(mp, d), x_sample.reshape(ms, d)], axis=0)
    row_cond = jnp.concatenate([jnp.zeros((mp,), jnp.int32),
                                1 + jnp.arange(ms, dtype=jnp.int32) // ts])
    rsel = (row_cond[:, None] == jnp.arange(MOD_ROWS, dtype=jnp.int32)[None, :]).astype(BF16)
    c_all = jnp.concatenate([c_prompt, c_sample,
                             jnp.zeros((MOD_ROWS - bp - bs, d), F32)], axis=0)
    mod = _modulation(c_all, w_ada, b_ada)

    zeros = lambda *s: jnp.zeros(s, F32)
    pad_m = lambda mm: jnp.pad(mm, ((0, 0), (0, LANES - mm.shape[1])))[:, None, :]
    outs_p = [[] for _ in range(6)]
    outs_s = [[] for _ in range(6)]

    w_br_a16, w_br_b16, w_br_c16 = w_br_a.astype(BF16), w_br_b.astype(BF16), w_br_c.astype(BF16)
    w_out16 = w_out.astype(BF16)

    for l in range(depth):
        mod_l = mod[l:l + 1]
        h = _norm_mod(x, rsel, norm_mix[l], mod_l, 0, 1)
        ps = [_matmul(h, w_in[l, :, s0:s1].astype(BF16), F32, TM_BIG, TN_PROJ)
              for s0, s1, _ in ranges]
        w_gw = jnp.concatenate([w_in[l, :, s:s + LANES] for s in gate_windows], axis=1)
        p_gw = _matmul(h, w_gw.astype(BF16), F32, TM_BIG, 2 * LANES)

        def at(nm, extra=0):
            ri, c = where[nm]
            return ps[ri], c + extra

        bias_if = jnp.pad(b_gate_a[l].reshape(1, 2 * nh_a), ((0, 0), (0, LANES - 2 * nh_a)))
        w_gk2p = jnp.pad(w_gk2[l], ((lr_lane, LANES - lr_lane - gate_rank), (0, 0)))
        groups = (
            dict(row0=0, B=bp, T=tp, La=64, Lg=64,
                 conv0=zeros(bp, CONV_W - 1, 2 * qk_a), c0=zeros(bp, nh_a, dv_a, dk_a),
                 n0=zeros(bp, nh_a, dk_a), m0=zeros(bp, 1, LANES),
                 sb0=zeros(bp, nh_b, dv_b, dk_b), sc0=zeros(bp, nh_c, dv_c, dk_c)),
            dict(row0=mp, B=bs, T=ts, La=ts, Lg=ts,
                 conv0=state_a_conv[l], c0=state_a_C[l], n0=state_a_n[l],
                 m0=pad_m(state_a_m[l]),
                 sb0=jnp.swapaxes(state_b_S[l], -1, -2), sc0=jnp.swapaxes(state_c_S[l], -1, -2)),
        )
        o_a, o_b, o_c = (jnp.zeros((m, w), BF16) for w in (w_a, w_b, w_c))
        for g, outs in zip(groups, (outs_p, outs_s)):
            o_a, a_c, a_n, a_m, a_conv = _mlstm(
                (at("a_qk"), at("a_qk", qk_a), at("a_v"), at("a_o"), (p_gw, 0)),
                g["conv0"], g["c0"], g["n0"], g["m0"], conv_w[l], conv_b[l].reshape(1, -1),
                bias_if, hn_a[l].reshape(1, -1), o_a,
                row0=g["row0"], B=g["B"], T=g["T"], L=g["La"], NH=nh_a, DK=dk_a, DV=dv_a)
            o_b, b_st = _gla_b(
                (at("b_q"), at("b_k"), at("b_v"), at("b_g"), (p_gw, LANES)),
                g["sb0"], w_gk2p, b_gk[l].reshape(1, -1), hn_b[l].reshape(1, -1), o_b,
                row0=g["row0"], B=g["B"], T=g["T"], L=g["Lg"], NH=nh_b, DK=dk_b, DV=dv_b, HP=4)
            o_c, c_st = _gla_c(
                (at("c_q"), at("c_f"), at("c_i"), at("c_g")),
                g["sc0"], lb_logits, hn_c[l].reshape(1, -1), o_c,
                layer=l, row0=g["row0"], B=g["B"], T=g["T"], L=g["Lg"],
                NH=nh_c, DK=dk_c, DV=dv_c, HP=8)
            for lst, val in zip(outs, (a_c, a_n, a_m[:, 0, :nh_a], a_conv,
                                       jnp.swapaxes(b_st, -1, -2), jnp.swapaxes(c_st, -1, -2))):
                lst.append(val)

        merged = _merge(o_a, o_b, o_c, w_br_a16, w_br_b16, w_br_c16, l, *at("gates"))
        x = _out_proj(merged, w_out16, l, x, rsel, mod_l, 2)

        w_r = jnp.pad(jnp.concatenate([w_rg[l], w_re[l]], axis=1),
                      ((0, 0), (0, LANES - N_GROUPS - N_EXPERTS)))
        b_r = jnp.pad(jnp.concatenate([b_rg[l], b_re[l]]),
                      (0, LANES - N_GROUPS - N_EXPERTS)).reshape(1, LANES)
        h2, logits = _norm_mod_router(x, rsel, norm_moe[l], mod_l, 3, 4, w_r, b_r)
        eid, wts = _route(logits)
        rows, roww, blk_e, n_act, slot = _dispatch(eid, wts)
        yb = _expert_mlp(h2[rows], roww, blk_e, n_act, w_exp1, w_exp3, w_exp2, l)
        x = _gated_add(x, yb[slot.reshape(-1)].reshape(m, TOP_K * d), rsel, mod_l, 5)

    y_prompt = _final_norm(x, norm_final, 0, mp, 512).reshape(bp, tp, d)
    y_sample = _final_norm(x, norm_final, mp, ms, ms).reshape(bs, ts, d)
    stack = lambda lst: jnp.stack(lst)
    return (y_prompt, y_sample,
            stack(outs_p[0]), stack(outs_p[1]), stack(outs_p[2]), stack(outs_p[3]),
            stack(outs_p[4]), stack(outs_p[5]),
            stack(outs_s[0]), stack(outs_s[1]), stack(outs_s[2]), stack(outs_s[3]),
            stack(outs_s[4]), stack(outs_s[5]))
```
